```python
import jax, jax.numpy as jnp
from jax import lax
import numpy as np

D_MODEL = 4096
BATCH = 2
SEQ = 8192
DEPTH = 1

GRID_W = 64
CTX_LEN = 256
HEAD_DIM = 64
ATTN_WIDTH = D_MODEL // 2
N_Q_HEADS = ATTN_WIDTH // HEAD_DIM
N_KV_HEADS = max(1, N_Q_HEADS // 8)
KV_GROUP = N_Q_HEADS // N_KV_HEADS
KV_WIDTH = N_KV_HEADS * HEAD_DIM
POOL_WIDTH = D_MODEL - ATTN_WIDTH
POOL_WINDOWS = (2, 4, 8, 16)
N_POOL_GROUPS = len(POOL_WINDOWS)
POOL_GROUP_WIDTH = POOL_WIDTH // N_POOL_GROUPS
MIX_WIDTH = ATTN_WIDTH + POOL_WIDTH
IN_WIDTH = ATTN_WIDTH + 2 * KV_WIDTH + POOL_WIDTH
WINDOW = 128
BLOCK = 128
ROPE_HALF = HEAD_DIM // 2
ROPE_BASE = 10000.0
N_EXPERTS = 32
TOP_K = 4
EXPERT_FF = D_MODEL // 4
SWIGLU_LIMIT = 7.0
SWIGLU_ALPHA = 1.702
N_MOD = 6
NORM_EPS = 1e-6
NEG_INF = -1e30

kernel_name = 'hybrid_window_gqa_pool_moe_dit_layer'


def rms_norm(x, g):
    xf = x.astype(jnp.float32)
    y = xf * lax.rsqrt(jnp.mean(xf * xf, axis=-1, keepdims=True) + NORM_EPS)
    return (y * g.astype(jnp.float32)).astype(x.dtype)


def modulate(h, shift, scale):
    return h * (1.0 + scale) + shift


def _rotate(x, ang):
    n = x.shape[-1] // 2
    cos = jnp.cos(ang)[None, :, None, :]
    sin = jnp.sin(ang)[None, :, None, :]
    x1, x2 = x[..., :n], x[..., n:]
    return jnp.concatenate([x1 * cos - x2 * sin, x2 * cos + x1 * sin], axis=-1)


def axial_rope(x, ang_row, ang_col):
    xf = x.astype(jnp.float32)
    out = jnp.concatenate([_rotate(xf[..., :ROPE_HALF], ang_row),
                           _rotate(xf[..., ROPE_HALF:], ang_col)], axis=-1)
    return out.astype(x.dtype)


def project(h, w_in, q_norm_g, k_norm_g):
    z = h @ w_in
    q, k, v, u = jnp.split(z, [ATTN_WIDTH, ATTN_WIDTH + KV_WIDTH, ATTN_WIDTH + 2 * KV_WIDTH], axis=-1)
    lead = h.shape[:-1]
    q = rms_norm(q.reshape(*lead, N_Q_HEADS, HEAD_DIM), q_norm_g)
    k = rms_norm(k.reshape(*lead, N_KV_HEADS, HEAD_DIM), k_norm_g)
    v = v.reshape(*lead, N_KV_HEADS, HEAD_DIM)
    return q, k, v, u


def latent_window_attention(q, k, v, k_ctx, v_ctx, sinks):
    b, s = q.shape[0], q.shape[1]
    nb = s // BLOCK
    scale = HEAD_DIM ** -0.5
    qb = q.reshape(b, nb, BLOCK, N_KV_HEADS, KV_GROUP, HEAD_DIM)

    def bands(t):
        tp = jnp.pad(t, ((0, 0), (BLOCK, BLOCK), (0, 0), (0, 0))).reshape(b, nb + 2, BLOCK, N_KV_HEADS, HEAD_DIM)
        return jnp.concatenate([tp[:, :-2], tp[:, 1:-1], tp[:, 2:]], axis=2)

    kb, vb = bands(k), bands(v)
    s_loc = jnp.einsum('bnqhgd,bnkhd->bnhgqk', qb, kb).astype(jnp.float32) * scale
    s_ctx = jnp.einsum('bnqhgd,bchd->bnhgqc', qb, k_ctx).astype(jnp.float32) * scale
    blk = jnp.arange(nb)[:, None] * BLOCK
    qpos = blk + jnp.arange(BLOCK)[None, :]
    kpos = blk - BLOCK + jnp.arange(3 * BLOCK)[None, :]
    kp = kpos[:, None, :]
    valid = (kp >= 0) & (kp < s) & (jnp.abs(qpos[:, :, None] - kp) <= WINDOW)
    s_loc = jnp.where(valid[None, :, None, None], s_loc, NEG_INF)
    sink = sinks.astype(jnp.float32).reshape(N_KV_HEADS, KV_GROUP)[None, None, :, :, None, None]
    m = jnp.maximum(jnp.maximum(s_loc.max(-1, keepdims=True), s_ctx.max(-1, keepdims=True)), sink)
    e_loc = jnp.exp(s_loc - m)
    e_ctx = jnp.exp(s_ctx - m)
    denom = e_loc.sum(-1, keepdims=True) + e_ctx.sum(-1, keepdims=True) + jnp.exp(sink - m)
    p_loc = (e_loc / denom).astype(v.dtype)
    p_ctx = (e_ctx / denom).astype(v.dtype)
    o = (jnp.einsum('bnhgqk,bnkhd->bnqhgd', p_loc, vb)
         + jnp.einsum('bnhgqc,bchd->bnqhgd', p_ctx, v_ctx))
    return o.reshape(b, s, ATTN_WIDTH)


def context_attention(q, k, v, sinks):
    b, n = q.shape[0], q.shape[1]
    qg = q.reshape(b, n, N_KV_HEADS, KV_GROUP, HEAD_DIM)
    sc = jnp.einsum('bqhgd,bkhd->bhgqk', qg, k).astype(jnp.float32) * HEAD_DIM ** -0.5
    sink = sinks.astype(jnp.float32).reshape(N_KV_HEADS, KV_GROUP)[None, :, :, None, None]
    m = jnp.maximum(sc.max(-1, keepdims=True), sink)
    e = jnp.exp(sc - m)
    p = (e / (e.sum(-1, keepdims=True) + jnp.exp(sink - m))).astype(v.dtype)
    o = jnp.einsum('bhgqk,bkhd->bqhgd', p, v)
    return o.reshape(b, n, ATTN_WIDTH)


def multiscale_pool(u, w_pool, pool_scale):
    b, n, _ = u.shape
    uf = u.astype(jnp.float32)
    cs = jnp.pad(jnp.cumsum(uf, axis=1), ((0, 0), (1, 0), (0, 0)))
    t = jnp.arange(n)
    diffs = []
    for g, w in enumerate(POOL_WINDOWS):
        lo = jnp.clip(t - w // 2, 0, n)
        hi = jnp.clip(t + w // 2, 0, n)
        csg = cs[..., g * POOL_GROUP_WIDTH:(g + 1) * POOL_GROUP_WIDTH]
        mean = (jnp.take(csg, hi, axis=1) - jnp.take(csg, lo, axis=1)) / (hi - lo).astype(jnp.float32)[None, :, None]
        diffs.append(mean - uf[..., g * POOL_GROUP_WIDTH:(g + 1) * POOL_GROUP_WIDTH])
    d = jnp.stack(diffs, axis=2).astype(u.dtype)
    y = jnp.einsum('blgc,gcd->blgd', d, w_pool).reshape(b, n, POOL_WIDTH)
    return y * pool_scale


def moe_ffn(h, w_router, b_router, w_gate, b_gate, w_up, b_up, w_down, b_down):
    hf = h.reshape(-1, h.shape[-1])
    logits = (hf @ w_router + b_router).astype(jnp.float32)
    top_logit, top_idx = lax.top_k(logits, TOP_K)
    top_w = jax.nn.softmax(top_logit, axis=-1)
    combine = jnp.einsum('nk,nke->en', top_w, jax.nn.one_hot(top_idx, N_EXPERTS, dtype=jnp.float32)).astype(h.dtype)
    out = jnp.zeros_like(hf)
    for e in range(N_EXPERTS):
        gate = jnp.minimum(hf @ w_gate[e] + b_gate[e], SWIGLU_LIMIT)
        lin = jnp.clip(hf @ w_up[e] + b_up[e], -SWIGLU_LIMIT, SWIGLU_LIMIT)
        act = gate * jax.nn.sigmoid(SWIGLU_ALPHA * gate) * (lin + 1.0)
        out = out + combine[e][:, None] * (act @ w_down[e] + b_down[e])
    return out.reshape(h.shape)


def setup_inputs(seed: int = 0) -> dict:
    key = jax.random.key(seed)
    ks = jax.random.split(key, 23)

    def nrm(k, shape, scale):
        return jax.random.normal(k, shape, jnp.float32) * scale

    return {
        'x': nrm(ks[0], (BATCH, SEQ, D_MODEL), 1.0),
        'c': nrm(ks[1], (BATCH, D_MODEL), 1.0),
        'ctx': nrm(ks[2], (BATCH, CTX_LEN, D_MODEL), 1.0),
        'c_ctx': nrm(ks[3], (D_MODEL,), 1.0),
        'w_ada': nrm(ks[4], (DEPTH, D_MODEL, N_MOD * D_MODEL), 0.5 * D_MODEL ** -0.5),
        'b_ada': nrm(ks[5], (DEPTH, N_MOD * D_MODEL), 0.02),
        'norm1_g': 1.0 + nrm(ks[6], (DEPTH, D_MODEL), 0.02),
        'w_in': nrm(ks[7], (DEPTH, D_MODEL, IN_WIDTH), D_MODEL ** -0.5),
        'q_norm_g': 1.0 + nrm(ks[8], (DEPTH, HEAD_DIM), 0.02),
        'k_norm_g': 1.0 + nrm(ks[9], (DEPTH, HEAD_DIM), 0.02),
        'sinks': nrm(ks[10], (DEPTH, N_Q_HEADS), 0.5),
        'w_pool': nrm(ks[11], (DEPTH, N_POOL_GROUPS, POOL_GROUP_WIDTH, POOL_GROUP_WIDTH), POOL_GROUP_WIDTH ** -0.5),
        'pool_scale': 1.0 + nrm(ks[12], (DEPTH, POOL_WIDTH), 0.1),
        'w_out': nrm(ks[13], (DEPTH, MIX_WIDTH, D_MODEL), MIX_WIDTH ** -0.5),
        'norm2_g': 1.0 + nrm(ks[14], (DEPTH, D_MODEL), 0.02),
        'w_router': nrm(ks[15], (DEPTH, D_MODEL, N_EXPERTS), D_MODEL ** -0.5),
        'b_router': nrm(ks[16], (DEPTH, N_EXPERTS), 0.01),
        'w_gate': nrm(ks[17], (DEPTH, N_EXPERTS, D_MODEL, EXPERT_FF), D_MODEL ** -0.5),
        'b_gate': nrm(ks[18], (DEPTH, N_EXPERTS, EXPERT_FF), 0.01),
        'w_up': nrm(ks[19], (DEPTH, N_EXPERTS, D_MODEL, EXPERT_FF), D_MODEL ** -0.5),
        'b_up': nrm(ks[20], (DEPTH, N_EXPERTS, EXPERT_FF), 0.01),
        'w_down': nrm(ks[21], (DEPTH, N_EXPERTS, EXPERT_FF, D_MODEL), EXPERT_FF ** -0.5),
        'b_down': nrm(ks[22], (DEPTH, N_EXPERTS, D_MODEL), 0.01),
    }


def reference(x, c, ctx, c_ctx, w_ada, b_ada, norm1_g, w_in, q_norm_g, k_norm_g, sinks,
              w_pool, pool_scale, w_out, norm2_g, w_router, b_router,
              w_gate, b_gate, w_up, b_up, w_down, b_down):
    n_tok = x.shape[1]
    rows = n_tok // GRID_W
    row = jnp.repeat(jnp.arange(rows, dtype=jnp.float32), GRID_W)
    col = jnp.tile(jnp.arange(GRID_W, dtype=jnp.float32), rows)
    inv_freq = ROPE_BASE ** (-jnp.arange(0, ROPE_HALF, 2, dtype=jnp.float32) / ROPE_HALF)
    ang_row = row[:, None] * inv_freq[None, :]
    ang_col = col[:, None] * inv_freq[None, :]

    for layer in range(DEPTH):
        last = layer == DEPTH - 1
        mod = (jax.nn.silu(c) @ w_ada[layer] + b_ada[layer]).reshape(c.shape[0], N_MOD, 1, D_MODEL)
        mod_c = (jax.nn.silu(c_ctx) @ w_ada[layer] + b_ada[layer]).reshape(N_MOD, D_MODEL)

        h_lat = modulate(rms_norm(x, norm1_g[layer]), mod[:, 0], mod[:, 1])
        h_ctx = modulate(rms_norm(ctx, norm1_g[layer]), mod_c[0], mod_c[1])
        q, k, v, u = project(h_lat, w_in[layer], q_norm_g[layer], k_norm_g[layer])
        q_c, k_c, v_c, u_c = project(h_ctx, w_in[layer], q_norm_g[layer], k_norm_g[layer])
        q = axial_rope(q, ang_row, ang_col)
        k = axial_rope(k, ang_row, ang_col)
        attn = latent_window_attention(q, k, v, k_c, v_c, sinks[layer])
        pool = multiscale_pool(u, w_pool[layer], pool_scale[layer])
        x = x + mod[:, 2] * (jnp.concatenate([attn, pool], axis=-1) @ w_out[layer])
        if not last:
            attn_c = context_attention(q_c, k_c, v_c, sinks[layer])
            pool_c = multiscale_pool(u_c, w_pool[layer], pool_scale[layer])
            ctx = ctx + mod_c[2] * (jnp.concatenate([attn_c, pool_c], axis=-1) @ w_out[layer])

        h2 = modulate(rms_norm(x, norm2_g[layer]), mod[:, 3], mod[:, 4])
        x = x + mod[:, 5] * moe_ffn(h2, w_router[layer], b_router[layer], w_gate[layer], b_gate[layer],
                                    w_up[layer], b_up[layer], w_down[layer], b_down[layer])
        if not last:
            h2_c = modulate(rms_norm(ctx, norm2_g[layer]), mod_c[3], mod_c[4])
            ctx = ctx + mod_c[5] * moe_ffn(h2_c, w_router[layer], b_router[layer], w_gate[layer], b_gate[layer],
                                           w_up[layer], b_up[layer], w_down[layer], b_down[layer])
    return x
```

```python
import functools

import jax
import jax.numpy as jnp
from jax import lax
from jax.experimental import pallas as pl
from jax.experimental.pallas import tpu as pltpu

F32 = jnp.float32
BF16 = jnp.bfloat16
U32 = jnp.uint32
I32 = jnp.int32

HEAD_DIM = 64
LANES = 128
GRID_W = 64
WINDOW = 128
BLOCK = 128
ROPE_HALF = HEAD_DIM // 2
ROPE_BASE = 10000.0
POOL_WINDOWS = (2, 4, 8, 16)
POOL_HALO = 8
TOP_K = 4
N_MOD = 6
NORM_EPS = 1e-6
NEG_INF = -1e30
SWIGLU_LIMIT = 7.0
SWIGLU_ALPHA = 1.702
VMEM_LIMIT = 56 * 1024 * 1024


def _cparams(sem):
    return pltpu.CompilerParams(dimension_semantics=sem, vmem_limit_bytes=VMEM_LIMIT)


def _pack_pair(lo, hi):
    a = pltpu.bitcast(lo.astype(BF16).astype(F32), U32) >> 16
    b = pltpu.bitcast(hi.astype(BF16).astype(F32), U32) & jnp.uint32(0xFFFF0000)
    return a | b


def _unpack_pair(w):
    lo = pltpu.bitcast(w << 16, F32)
    hi = pltpu.bitcast(w & jnp.uint32(0xFFFF0000), F32)
    return lo, hi


def _ada_kernel(c_ref, w_ref, b_ref, o_ref):
    c = c_ref[...]
    s = c * jax.nn.sigmoid(c)
    hi = s.astype(BF16)
    lo = (s - hi.astype(F32)).astype(BF16)
    lhs = jnp.concatenate([hi, lo], axis=0)
    r = jnp.dot(lhs, w_ref[...].astype(BF16), preferred_element_type=F32)
    o_ref[...] = r[:8] + r[8:] + b_ref[...]


def _ada(c8, w_ada, b_ada):
    d, n = w_ada.shape
    bn = min(512, n)
    return pl.pallas_call(
        _ada_kernel,
        grid=(n // bn,),
        in_specs=[pl.BlockSpec((8, d), lambda j: (0, 0)),
                  pl.BlockSpec((d, bn), lambda j: (0, j)),
                  pl.BlockSpec((1, bn), lambda j: (0, j))],
        out_specs=pl.BlockSpec((8, bn), lambda j: (0, j)),
        out_shape=jax.ShapeDtypeStruct((8, n), F32),
        compiler_params=_cparams(("parallel",)),
        name="ada",
    )(c8, w_ada, b_ada)


def _norm_mod(x, g, shift, scale):
    ms = jnp.mean(x * x, axis=-1, keepdims=True)
    y = x * lax.rsqrt(ms + NORM_EPS) * g
    return y * (1.0 + scale) + shift


def _head_norm(z, bd, gain):
    ssq = jnp.dot((z * z).astype(BF16), bd, preferred_element_type=F32)
    return z * lax.rsqrt(ssq * (1.0 / HEAD_DIM) + NORM_EPS) * gain


def _rope(z, cos, sin):
    rows = z.shape[0]
    lane = lax.broadcasted_iota(I32, (rows, LANES), 1)
    first = (lane % ROPE_HALF) < (ROPE_HALF // 2)
    outs = []
    for s in range(z.shape[1] // LANES):
        zs = z[:, s * LANES:(s + 1) * LANES]
        up = pltpu.roll(zs, LANES - ROPE_HALF // 2, 1)
        dn = pltpu.roll(zs, ROPE_HALF // 2, 1)
        outs.append(zs * cos + jnp.where(first, up, dn) * sin)
    return outs[0] if len(outs) == 1 else jnp.concatenate(outs, axis=1)


def _tile_gain(g_ref, width):
    g = g_ref[...]
    return g if width == LANES else jnp.concatenate([g] * (width // LANES), axis=1)


def _inproj_kernel(x_ref, sh_ref, sc_ref, g1_ref, w_ref, qg_ref, kg_ref, cos_ref, sin_ref, bd_ref,
                   q_ref, kv_ref, u_ref, h_ref, *, nq_t, kvw):
    j = pl.program_id(1)

    @pl.when(j == 0)
    def _():
        h_ref[...] = _norm_mod(x_ref[...], g1_ref[...], sh_ref[...], sc_ref[...]).astype(BF16)

    z = jnp.dot(h_ref[...], w_ref[...], preferred_element_type=F32)
    bn = z.shape[1]

    @pl.when(j < nq_t)
    def _():
        zn = _head_norm(z, bd_ref[...], _tile_gain(qg_ref, bn))
        q_ref[...] = (_rope(zn, cos_ref[...], sin_ref[...]) * (HEAD_DIM ** -0.5)).astype(BF16)

    @pl.when(j == nq_t)
    def _():
        zk = z[:, :kvw]
        kn = _head_norm(zk, bd_ref[:kvw, :kvw], _tile_gain(kg_ref, kvw))
        kv_ref[:, :kvw] = _rope(kn, cos_ref[...], sin_ref[...]).astype(BF16)
        kv_ref[:, kvw:] = z[:, kvw:].astype(BF16)

    @pl.when(j > nq_t)
    def _():
        u_ref[...] = z


def _inproj(x2, shift, scale, g1, w_in_b, qg, kg, cos_t, sin_t, bd, *, seq, aw, kvw, pw):
    m, d = x2.shape
    bn = 2 * kvw
    nq_t, nu_t = aw // bn, pw // bn
    bm = min(512, seq)
    tpb = seq // bm
    kern = functools.partial(_inproj_kernel, nq_t=nq_t, kvw=kvw)
    return pl.pallas_call(
        kern,
        grid=(m // bm, nq_t + 1 + nu_t),
        in_specs=[pl.BlockSpec((bm, d), lambda i, j: (i, 0)),
                  pl.BlockSpec((None, 1, d), lambda i, j: (i // tpb, 0, 0)),
                  pl.BlockSpec((None, 1, d), lambda i, j: (i // tpb, 0, 0)),
                  pl.BlockSpec((1, d), lambda i, j: (0, 0)),
                  pl.BlockSpec((d, bn), lambda i, j: (0, j)),
                  pl.BlockSpec((1, LANES), lambda i, j: (0, 0)),
                  pl.BlockSpec((1, LANES), lambda i, j: (0, 0)),
                  pl.BlockSpec((bm, LANES), lambda i, j: (i % tpb, 0)),
                  pl.BlockSpec((bm, LANES), lambda i, j: (i % tpb, 0)),
                  pl.BlockSpec((bn, bn), lambda i, j: (0, 0))],
        out_specs=[pl.BlockSpec((bm, bn), lambda i, j: (i, jnp.minimum(j, nq_t - 1))),
                   pl.BlockSpec((bm, bn), lambda i, j: (i, 0)),
                   pl.BlockSpec((bm, bn), lambda i, j: (i, jnp.clip(j - nq_t - 1, 0, nu_t - 1)))],
        out_shape=[jax.ShapeDtypeStruct((m, aw), BF16),
                   jax.ShapeDtypeStruct((m, 2 * kvw), BF16),
                   jax.ShapeDtypeStruct((m, pw), F32)],
        scratch_shapes=[pltpu.VMEM((bm, d), BF16)],
        compiler_params=_cparams(("parallel", "arbitrary")),
        name="inproj",
    )(x2, shift, scale, g1, w_in_b, qg, kg, cos_t, sin_t, bd)


def _ctx_kv_kernel(x_ref, sh_ref, sc_ref, g1_ref, w_ref, kg_ref, bd_ref, kv_ref, *, kvw):
    h = _norm_mod(x_ref[...], g1_ref[...], sh_ref[...], sc_ref[...]).astype(BF16)
    z = jnp.dot(h, w_ref[...], preferred_element_type=F32)
    kn = _head_norm(z[:, :kvw], bd_ref[:kvw, :kvw], _tile_gain(kg_ref, kvw))
    kv_ref[:, :kvw] = kn.astype(BF16)
    kv_ref[:, kvw:] = z[:, kvw:].astype(BF16)


def _ctx_kv(c2, shift, scale, g1, w_kv_b, kg, bd, *, kvw):
    m, d = c2.shape
    bn = 2 * kvw
    bm = min(256, m)
    kern = functools.partial(_ctx_kv_kernel, kvw=kvw)
    return pl.pallas_call(
        kern,
        grid=(m // bm,),
        in_specs=[pl.BlockSpec((bm, d), lambda i: (i, 0)),
                  pl.BlockSpec((1, d), lambda i: (0, 0)),
                  pl.BlockSpec((1, d), lambda i: (0, 0)),
                  pl.BlockSpec((1, d), lambda i: (0, 0)),
                  pl.BlockSpec((d, bn), lambda i: (0, 0)),
                  pl.BlockSpec((1, LANES), lambda i: (0, 0)),
                  pl.BlockSpec((bn, bn), lambda i: (0, 0))],
        out_specs=pl.BlockSpec((bm, bn), lambda i: (i, 0)),
        out_shape=jax.ShapeDtypeStruct((m, bn), BF16),
        compiler_params=_cparams(("parallel",)),
        name="ctx_kv",
    )(c2, shift, scale, g1, w_kv_b, kg, bd)


def _attn_kernel(sink_ref, q_ref, kp_ref, kc_ref, kn_ref, kx_ref, bias_ref, o_ref, *, nkv, grp, kvw, ctx_len):
    npair = grp // 2
    nk = ctx_len + 3 * BLOCK
    lane_kv = lax.broadcasted_iota(I32, (nk, LANES), 1)
    lane_q = lax.broadcasted_iota(I32, (BLOCK, LANES), 1)
    bias = bias_ref[...]
    for h in range(nkv):
        slab = (h // 2) * LANES
        mine = (lane_kv // HEAD_DIM) == (h % 2)

        def window(off):
            w = jnp.concatenate([kx_ref[:, off + slab:off + slab + LANES],
                                 kp_ref[:, off + slab:off + slab + LANES],
                                 kc_ref[:, off + slab:off + slab + LANES],
                                 kn_ref[:, off + slab:off + slab + LANES]], axis=0).astype(F32)
            wz = jnp.where(mine, w, 0.0)
            wr = pltpu.roll(wz, HEAD_DIM, 1)
            lo, hi = (wz, wr) if h % 2 == 0 else (wr, wz)
            return lo.astype(BF16), hi.astype(BF16)

        k_e, k_o = window(0)
        v_e, v_o = window(kvw)
        q0 = h * grp * HEAD_DIM
        q4 = jnp.concatenate([q_ref[:, q0 + p * LANES:q0 + (p + 1) * LANES] for p in range(npair)], axis=0)
        kcat = jnp.concatenate([k_e, k_o], axis=0)
        s_all = lax.dot_general(q4, kcat, (((1,), (1,)), ((), ())), preferred_element_type=F32)
        p_e, p_o, rinv = [], [], []
        for p in range(npair):
            halves = []
            for half in range(2):
                sink = sink_ref[h * grp + 2 * p + half]
                s = s_all[p * BLOCK:(p + 1) * BLOCK, half * nk:(half + 1) * nk]
                s_ctx = s[:, :ctx_len]
                s_loc = s[:, ctx_len:] + bias
                mx = jnp.maximum(jnp.maximum(jnp.max(s_ctx, axis=-1, keepdims=True),
                                             jnp.max(s_loc, axis=-1, keepdims=True)), sink)
                e_ctx = jnp.exp(s_ctx - mx)
                e_loc = jnp.exp(s_loc - mx)
                den = (jnp.sum(e_ctx, axis=-1, keepdims=True) + jnp.sum(e_loc, axis=-1, keepdims=True)
                       + jnp.exp(sink - mx))
                halves.append((jnp.concatenate([e_ctx, e_loc], axis=1).astype(BF16), 1.0 / den))
            p_e.append(halves[0][0])
            p_o.append(halves[1][0])
            rinv.append(jnp.where(lane_q < HEAD_DIM, halves[0][1], halves[1][1]))
        o = (jnp.dot(jnp.concatenate(p_e, axis=0), v_e, preferred_element_type=F32)
             + jnp.dot(jnp.concatenate(p_o, axis=0), v_o, preferred_element_type=F32))
        for p in range(npair):
            o_ref[:, q0 + p * LANES:q0 + (p + 1) * LANES] = (o[p * BLOCK:(p + 1) * BLOCK] * rinv[p]).astype(BF16)


def _attn(q, kv, kvc, bias, sinks, *, batch, seq, ctx_len, nkv, grp, kvw):
    m, aw = q.shape
    nb = seq // BLOCK
    kern = functools.partial(_attn_kernel, nkv=nkv, grp=grp, kvw=kvw, ctx_len=ctx_len)

    def sel(n):
        first = 1 - jnp.minimum(n, 1)
        last = 1 - jnp.minimum(nb - 1 - n, 1)
        return 1 - first + last

    return pl.pallas_call(
        kern,
        grid=(batch, nb),
        in_specs=[pl.BlockSpec(memory_space=pltpu.SMEM),
                  pl.BlockSpec((BLOCK, aw), lambda b, n: (b * nb + n, 0)),
                  pl.BlockSpec((BLOCK, 2 * kvw), lambda b, n: (b * nb + jnp.maximum(n - 1, 0), 0)),
                  pl.BlockSpec((BLOCK, 2 * kvw), lambda b, n: (b * nb + n, 0)),
                  pl.BlockSpec((BLOCK, 2 * kvw), lambda b, n: (b * nb + jnp.minimum(n + 1, nb - 1), 0)),
                  pl.BlockSpec((ctx_len, 2 * kvw), lambda b, n: (b, 0)),
                  pl.BlockSpec((None, BLOCK, 3 * BLOCK), lambda b, n: (sel(n), 0, 0))],
        out_specs=pl.BlockSpec((BLOCK, aw), lambda b, n: (b * nb + n, 0)),
        out_shape=jax.ShapeDtypeStruct((m, aw), BF16),
        compiler_params=_cparams(("parallel", "parallel")),
        name="attn",
    )(sinks, q, kv, kv, kv, kvc, bias)


def _pool_kernel(up_ref, uc_ref, un_ref, wp_ref, ps_ref, o_ref, *, seq, pgw):
    bm = uc_ref.shape[0]
    t0 = (pl.program_id(0) * bm) % seq
    rows = bm + 2 * POOL_HALO
    pt = t0 + lax.broadcasted_iota(I32, (bm, rows), 0)
    ps = t0 - POOL_HALO + lax.broadcasted_iota(I32, (bm, rows), 1)
    pt1 = t0 + lax.broadcasted_iota(I32, (bm, 1), 0)
    for g, w in enumerate(POOL_WINDOWS):
        cs = slice(g * pgw, (g + 1) * pgw)
        uc = uc_ref[:, cs]
        ucat = jnp.concatenate([up_ref[:, cs], uc, un_ref[:, cs]], axis=0)
        u_hi = ucat.astype(BF16)
        u_lo = (ucat - u_hi.astype(F32)).astype(BF16)
        lo = jnp.clip(pt - w // 2, 0, seq)
        hi = jnp.clip(pt + w // 2, 0, seq)
        band = jnp.where((ps >= lo) & (ps < hi), 1.0, 0.0).astype(BF16)
        cnt = (jnp.clip(pt1 + w // 2, 0, seq) - jnp.clip(pt1 - w // 2, 0, seq)).astype(F32)
        tot = (jnp.dot(band, u_hi, preferred_element_type=F32)
               + jnp.dot(band, u_lo, preferred_element_type=F32))
        dlt = tot / cnt - uc
        y = jnp.dot(dlt.astype(BF16), wp_ref[g], preferred_element_type=F32)
        o_ref[:, cs] = (y * ps_ref[:, cs]).astype(BF16)


def _pool(u, w_pool_b, pool_scale, *, seq):
    m, pw = u.shape
    pgw = pw // len(POOL_WINDOWS)
    bm = min(256, seq)
    hb = bm // POOL_HALO
    nh = m // POOL_HALO
    kern = functools.partial(_pool_kernel, seq=seq, pgw=pgw)
    return pl.pallas_call(
        kern,
        grid=(m // bm,),
        in_specs=[pl.BlockSpec((POOL_HALO, pw), lambda i: (jnp.maximum(i * hb - 1, 0), 0)),
                  pl.BlockSpec((bm, pw), lambda i: (i, 0)),
                  pl.BlockSpec((POOL_HALO, pw), lambda i: (jnp.minimum((i + 1) * hb, nh - 1), 0)),
                  pl.BlockSpec((len(POOL_WINDOWS), pgw, pgw), lambda i: (0, 0, 0)),
                  pl.BlockSpec((1, pw), lambda i: (0, 0))],
        out_specs=pl.BlockSpec((bm, pw), lambda i: (i, 0)),
        out_shape=jax.ShapeDtypeStruct((m, pw), BF16),
        compiler_params=_cparams(("parallel",)),
        name="pool",
    )(u, u, u, w_pool_b, pool_scale)


def _outproj_kernel(a_ref, p_ref, wa_ref, wp_ref, x_ref, g_ref, o_ref):
    mix = (jnp.dot(a_ref[...], wa_ref[...], preferred_element_type=F32)
           + jnp.dot(p_ref[...], wp_ref[...], preferred_element_type=F32))
    o_ref[...] = x_ref[...] + g_ref[...] * mix


def _outproj(attn, pool, w_out_b, x2, gate, *, seq):
    m, d = x2.shape
    aw = attn.shape[1]
    bm = min(1024, seq)
    bn = min(512, d)
    tpb = seq // bm
    return pl.pallas_call(
        _outproj_kernel,
        grid=(m // bm, d // bn),
        in_specs=[pl.BlockSpec((bm, aw), lambda i, j: (i, 0)),
                  pl.BlockSpec((bm, aw), lambda i, j: (i, 0)),
                  pl.BlockSpec((aw, bn), lambda i, j: (0, j)),
                  pl.BlockSpec((aw, bn), lambda i, j: (1, j)),
                  pl.BlockSpec((bm, bn), lambda i, j: (i, j)),
                  pl.BlockSpec((None, 1, bn), lambda i, j: (i // tpb, 0, j))],
        out_specs=pl.BlockSpec((bm, bn), lambda i, j: (i, j)),
        out_shape=jax.ShapeDtypeStruct((m, d), F32),
        compiler_params=_cparams(("parallel", "parallel")),
        name="outproj",
    )(attn, pool, w_out_b, w_out_b, x2, gate)


def _router_kernel(x_ref, sh_ref, sc_ref, g2_ref, wr_ref, br_ref, tri_ref,
                   hp_ref, mi_ref, mw_ref, cnt_ref, carry_ref, *, n_exp):
    i = pl.program_id(0)

    @pl.when(i == 0)
    def _():
        carry_ref[...] = jnp.zeros_like(carry_ref)

    h = _norm_mod(x_ref[...], g2_ref[...], sh_ref[...], sc_ref[...])
    bm, d = h.shape
    hp_ref[...] = _pack_pair(h[:, :d // 2], h[:, d // 2:])
    logits = lax.dot_general(wr_ref[...], h.astype(BF16), (((1,), (1,)), ((), ())),
                             preferred_element_type=F32) + br_ref[...]
    eidx = lax.broadcasted_iota(I32, (n_exp, bm), 0)
    l = logits
    tops, idxs, sels = [], [], []
    for _ in range(TOP_K):
        mx = jnp.max(l, axis=0, keepdims=True)
        ix = jnp.min(jnp.where(l == mx, eidx, n_exp), axis=0, keepdims=True)
        sel = eidx == ix
        tops.append(mx)
        idxs.append(ix)
        sels.append(sel)
        l = jnp.where(sel, -jnp.inf, l)
    es = [jnp.exp(t - tops[0]) for t in tops]
    den = es[0] + es[1] + es[2] + es[3]
    member = jnp.where(sels[0] | sels[1] | sels[2] | sels[3], 1.0, 0.0)
    before = jnp.dot(member.astype(BF16), tri_ref[...], preferred_element_type=F32) + carry_ref[:, 0:1]
    ranks = [jnp.sum(jnp.where(s, before, 0.0), axis=0, keepdims=True).astype(I32) for s in sels]
    carry_ref[...] = carry_ref[...] + jnp.sum(member, axis=1, keepdims=True)
    mi_ref[...] = jnp.concatenate(idxs + ranks, axis=0)
    mw_ref[...] = jnp.concatenate([e / den for e in es] + [jnp.zeros((TOP_K, bm), F32)], axis=0)
    cnt_ref[...] = carry_ref[...]


def _router(x1, shift, scale, g2, w_rt_b, b_r, tri, *, seq):
    m, d = x1.shape
    n_exp = w_rt_b.shape[0]
    bm = tri.shape[0]
    tpb = seq // bm
    kern = functools.partial(_router_kernel, n_exp=n_exp)
    return pl.pallas_call(
        kern,
        grid=(m // bm,),
        in_specs=[pl.BlockSpec((bm, d), lambda i: (i, 0)),
                  pl.BlockSpec((None, 1, d), lambda i: (i // tpb, 0, 0)),
                  pl.BlockSpec((None, 1, d), lambda i: (i // tpb, 0, 0)),
                  pl.BlockSpec((1, d), lambda i: (0, 0)),
                  pl.BlockSpec((n_exp, d), lambda i: (0, 0)),
                  pl.BlockSpec((n_exp, 1), lambda i: (0, 0)),
                  pl.BlockSpec((bm, bm), lambda i: (0, 0))],
        out_specs=[pl.BlockSpec((bm, d // 2), lambda i: (i, 0)),
                   pl.BlockSpec((2 * TOP_K, bm), lambda i: (0, i)),
                   pl.BlockSpec((2 * TOP_K, bm), lambda i: (0, i)),
                   pl.BlockSpec((n_exp, LANES), lambda i: (0, 0))],
        out_shape=[jax.ShapeDtypeStruct((m, d // 2), U32),
                   jax.ShapeDtypeStruct((2 * TOP_K, m), I32),
                   jax.ShapeDtypeStruct((2 * TOP_K, m), F32),
                   jax.ShapeDtypeStruct((n_exp, LANES), F32)],
        scratch_shapes=[pltpu.VMEM((n_exp, LANES), F32)],
        compiler_params=_cparams(("arbitrary",)),
        name="router",
    )(x1, shift, scale, g2, w_rt_b, b_r, tri)


def _row_copy(src, s_row, dst, d_row, sem):
    return pltpu.make_async_copy(src.at[pl.ds(s_row, 1)], dst.at[pl.ds(d_row, 1)], sem)


def _dispatch_kernel(pos_ref, h_ref, xs_ref, pos_s, sem_p, sem):
    bd = h_ref.shape[0]
    cp = pltpu.make_async_copy(pos_ref.at[0], pos_s, sem_p)
    cp.start()
    cp.wait()

    def issue(t, c):
        for s in range(TOP_K):
            _row_copy(h_ref, t, xs_ref, pos_s[s, t], sem).start()
        return c

    lax.fori_loop(0, bd, issue, 0)

    def drain(t, c):
        for s in range(TOP_K):
            _row_copy(h_ref, 0, xs_ref, 0, sem).wait()
        return c

    lax.fori_loop(0, bd, drain, 0)


def _dispatch(pos3, hp, *, rows):
    m, dh = hp.shape
    bd = pos3.shape[2]
    return pl.pallas_call(
        _dispatch_kernel,
        grid=(m // bd,),
        in_specs=[pl.BlockSpec((1, TOP_K, bd), lambda i: (i, 0, 0)),
                  pl.BlockSpec((bd, dh), lambda i: (i, 0))],
        out_specs=pl.BlockSpec(memory_space=pl.ANY),
        out_shape=jax.ShapeDtypeStruct((rows, dh), U32),
        scratch_shapes=[pltpu.SMEM((TOP_K, bd), I32), pltpu.SemaphoreType.DMA(()), pltpu.SemaphoreType.DMA(())],
        compiler_params=_cparams(("arbitrary",)),
        name="dispatch",
    )(pos3, hp)


def _moe_up_kernel(ie_ref, ic_ref, it_ref, if_ref, iv_ref, in_ref,
                   xs_ref, wg_ref, wu_ref, bg_ref, bu_ref, a_ref, wg_s, wu_s):
    l = pl.program_id(0)

    @pl.when(if_ref[l] == 1)
    def _():
        wg_s[...] = wg_ref[...].astype(BF16)
        wu_s[...] = wu_ref[...].astype(BF16)

    @pl.when(iv_ref[l] == 1)
    def _():
        w = xs_ref[...]
        row = lax.broadcasted_iota(I32, w.shape, 0)
        w = jnp.where(row < in_ref[l], w, jnp.uint32(0))
        lo, hi = _unpack_pair(w)
        x = jnp.concatenate([lo.astype(BF16), hi.astype(BF16)], axis=1)
        gate = jnp.minimum(jnp.dot(x, wg_s[...], preferred_element_type=F32) + bg_ref[...], SWIGLU_LIMIT)
        lin = jnp.clip(jnp.dot(x, wu_s[...], preferred_element_type=F32) + bu_ref[...],
                       -SWIGLU_LIMIT, SWIGLU_LIMIT)
        a_ref[...] = (gate * jax.nn.sigmoid(SWIGLU_ALPHA * gate) * (lin + 1.0)).astype(BF16)


def _moe_up(items, xs, w_gate, w_up, b_gate, b_up, *, bmoe, nck):
    rows, dh = xs.shape
    n_exp, d, ff = w_gate.shape
    fc = ff // nck
    n_items = items[0].shape[0]
    grid_spec = pltpu.PrefetchScalarGridSpec(
        num_scalar_prefetch=6,
        grid=(n_items,),
        in_specs=[pl.BlockSpec((bmoe, dh), lambda l, ie, ic, it, *_: (it[l], 0)),
                  pl.BlockSpec((None, d, fc), lambda l, ie, ic, it, *_: (ie[l], 0, ic[l])),
                  pl.BlockSpec((None, d, fc), lambda l, ie, ic, it, *_: (ie[l], 0, ic[l])),
                  pl.BlockSpec((None, 1, fc), lambda l, ie, ic, it, *_: (ie[l], 0, ic[l])),
                  pl.BlockSpec((None, 1, fc), lambda l, ie, ic, it, *_: (ie[l], 0, ic[l]))],
        out_specs=pl.BlockSpec((bmoe, fc), lambda l, ie, ic, it, *_: (it[l], ic[l])),
        scratch_shapes=[pltpu.VMEM((d, fc), BF16), pltpu.VMEM((d, fc), BF16)],
    )
    return pl.pallas_call(
        _moe_up_kernel,
        grid_spec=grid_spec,
        out_shape=jax.ShapeDtypeStruct((rows, ff), BF16),
        compiler_params=_cparams(("arbitrary",)),
        name="moe_up",
    )(*items, xs, w_gate, w_up, b_gate, b_up)


def _moe_down_kernel(ie_ref, ic_ref, it_ref, if_ref, iv_ref, in_ref, a_ref, wd_ref, bd_ref, y_ref, wd_s):
    l = pl.program_id(0)

    @pl.when(if_ref[l] == 1)
    def _():
        wd_s[...] = wd_ref[...].astype(BF16)

    @pl.when(iv_ref[l] == 1)
    def _():
        y = jnp.dot(a_ref[...], wd_s[...], preferred_element_type=F32) + bd_ref[...]
        half = y.shape[1] // 2
        y_ref[...] = _pack_pair(y[:, :half], y[:, half:])


def _moe_down(items, act, w_down, b_down, *, bmoe, nck):
    rows, ff = act.shape
    n_exp, _, d = w_down.shape
    dc = d // nck
    n_items = items[0].shape[0]
    grid_spec = pltpu.PrefetchScalarGridSpec(
        num_scalar_prefetch=6,
        grid=(n_items,),
        in_specs=[pl.BlockSpec((bmoe, ff), lambda l, ie, ic, it, *_: (it[l], 0)),
                  pl.BlockSpec((None, ff, dc), lambda l, ie, ic, it, *_: (ie[l], 0, ic[l])),
                  pl.BlockSpec((None, 1, dc), lambda l, ie, ic, it, *_: (ie[l], 0, ic[l]))],
        out_specs=pl.BlockSpec((bmoe, dc // 2), lambda l, ie, ic, it, *_: (it[l], ic[l])),
        scratch_shapes=[pltpu.VMEM((ff, dc), BF16)],
    )
    return pl.pallas_call(
        _moe_down_kernel,
        grid_spec=grid_spec,
        out_shape=jax.ShapeDtypeStruct((rows, d // 2), U32),
        compiler_params=_cparams(("arbitrary",)),
        name="moe_down",
    )(*items, act, w_down, b_down)


def _combine_kernel(pos_ref, w_ref, x_ref, g_ref, ys_ref, o_ref, pos_s, buf, sem_p, sem, *, nck):
    bc = x_ref.shape[0]
    cp = pltpu.make_async_copy(pos_ref.at[0], pos_s, sem_p)
    cp.start()
    cp.wait()

    def issue(t, c):
        for s in range(TOP_K):
            _row_copy(ys_ref, pos_s[s, t], buf.at[s], t, sem).start()
        return c

    lax.fori_loop(0, bc, issue, 0)

    def drain(t, c):
        for s in range(TOP_K):
            _row_copy(ys_ref, 0, buf.at[s], 0, sem).wait()
        return c

    lax.fori_loop(0, bc, drain, 0)

    dq = x_ref.shape[1] // (2 * nck)
    acc = [jnp.zeros((bc, dq), F32) for _ in range(2 * nck)]
    for s in range(TOP_K):
        ws = w_ref[:, s:s + 1]
        for c in range(nck):
            lo, hi = _unpack_pair(buf[s, :, c * dq:(c + 1) * dq])
            acc[2 * c] = acc[2 * c] + ws * lo
            acc[2 * c + 1] = acc[2 * c + 1] + ws * hi
    for k in range(2 * nck):
        cs = slice(k * dq, (k + 1) * dq)
        o_ref[:, cs] = x_ref[:, cs] + g_ref[:, cs] * acc[k]


def _combine(pos3, wtok, x1, gate, ys, *, seq, nck):
    m, d = x1.shape
    bc = pos3.shape[2]
    tpb = seq // bc
    kern = functools.partial(_combine_kernel, nck=nck)
    return pl.pallas_call(
        kern,
        grid=(m // bc,),
        in_specs=[pl.BlockSpec((1, TOP_K, bc), lambda i: (i, 0, 0)),
                  pl.BlockSpec((bc, 2 * TOP_K), lambda i: (i, 0)),
                  pl.BlockSpec((bc, d), lambda i: (i, 0)),
                  pl.BlockSpec((None, 1, d), lambda i: (i // tpb, 0, 0)),
                  pl.BlockSpec(memory_space=pl.ANY)],
        out_specs=pl.BlockSpec((bc, d), lambda i: (i, 0)),
        out_shape=jax.ShapeDtypeStruct((m, d), F32),
        scratch_shapes=[pltpu.SMEM((TOP_K, bc), I32), pltpu.VMEM((TOP_K, bc, d // 2), U32),
                        pltpu.SemaphoreType.DMA(()), pltpu.SemaphoreType.DMA(())],
        compiler_params=_cparams(("arbitrary",)),
        name="combine",
    )(pos3, wtok, x1, gate, ys)


def _rope_tables(seq):
    t = jnp.arange(seq)
    row = (t // GRID_W).astype(F32)
    col = (t % GRID_W).astype(F32)
    inv_freq = ROPE_BASE ** (-jnp.arange(0, ROPE_HALF, 2, dtype=F32) / ROPE_HALF)
    ar = row[:, None] * inv_freq[None, :]
    ac = col[:, None] * inv_freq[None, :]
    cos = jnp.concatenate([jnp.cos(ar), jnp.cos(ar), jnp.cos(ac), jnp.cos(ac)], axis=1)
    sin = jnp.concatenate([-jnp.sin(ar), jnp.sin(ar), -jnp.sin(ac), jnp.sin(ac)], axis=1)
    rep = LANES // HEAD_DIM
    return jnp.tile(cos, (1, rep)), jnp.tile(sin, (1, rep))


def _window_bias(seq):
    i = jnp.arange(BLOCK)[:, None]
    j = jnp.arange(3 * BLOCK)[None, :]
    band = (j >= i) & (j <= i + 2 * WINDOW)
    first = band & (j >= BLOCK)
    last = band & (j < 2 * BLOCK)
    both = first & last
    variants = jnp.stack([both if seq == BLOCK else first, band, last])
    return jnp.where(variants, 0.0, NEG_INF).astype(F32)


def _work_items(counts, *, bmoe, nck, n_tiles):
    n_exp = counts.shape[0]
    tiles_e = (counts + bmoe - 1) // bmoe
    tile_end = jnp.cumsum(tiles_e)
    tile_start = tile_end - tiles_e
    item_end = nck * tile_end
    total = item_end[-1]
    l = jnp.arange(nck * n_tiles, dtype=I32)
    valid = l < total
    lc = jnp.minimum(l, total - 1)
    e = jnp.minimum(jnp.searchsorted(item_end, lc, side="right"), n_exp - 1).astype(I32)
    nt = jnp.maximum(tiles_e[e], 1)
    within = lc - nck * tile_start[e]
    c = within // nt
    k = within % nt
    t = tile_start[e] + k
    first = (k == 0) & valid
    nrows = jnp.clip(counts[e] - k * bmoe, 0, bmoe)
    as_i = lambda a: a.astype(I32)
    return (as_i(e), as_i(c), as_i(t), as_i(first), as_i(valid), as_i(nrows)), tile_start * bmoe


def kernel(x, c, ctx, c_ctx, w_ada, b_ada, norm1_g, w_in, q_norm_g, k_norm_g, sinks, w_pool, pool_scale,
           w_out, norm2_g, w_router, b_router, w_gate, b_gate, w_up, b_up, w_down, b_down):
    assert w_ada.shape[0] == 1, "one layer: the context stream is only read, never updated"
    batch, seq, d = x.shape
    ctx_len = ctx.shape[1]
    aw = d // 2
    nq = aw // HEAD_DIM
    nkv = max(1, nq // 8)
    grp = nq // nkv
    kvw = nkv * HEAD_DIM
    pw = d - aw
    n_exp, _, ff = w_gate.shape[1:]
    m = batch * seq
    assert seq % BLOCK == 0 and seq % GRID_W == 0 and kvw % LANES == 0 and grp % 2 == 0 and aw == pw

    c8 = jnp.zeros((8, d), F32).at[:batch].set(c).at[batch].set(c_ctx)
    mod = _ada(c8, w_ada[0], b_ada)
    mod_b = mod[:batch].reshape(batch, N_MOD, 1, d)
    mod_c = mod[batch].reshape(N_MOD, 1, d)

    w_in_b = w_in[0].astype(BF16)
    w_out_b = w_out[0].astype(BF16)
    cos_t, sin_t = _rope_tables(seq)
    qg = jnp.tile(q_norm_g[0].astype(F32), LANES // HEAD_DIM)[None, :]
    kg = jnp.tile(k_norm_g[0].astype(F32), LANES // HEAD_DIM)[None, :]
    bn = 2 * kvw
    lane_head = jnp.arange(bn) // HEAD_DIM
    bd = (lane_head[:, None] == lane_head[None, :]).astype(BF16)
    g1 = norm1_g[0][None, :]

    x2 = x.reshape(m, d)
    q, kv, u = _inproj(x2, mod_b[:, 0], mod_b[:, 1], g1, w_in_b, qg, kg, cos_t, sin_t, bd,
                       seq=seq, aw=aw, kvw=kvw, pw=pw)
    kvc = _ctx_kv(ctx.reshape(batch * ctx_len, d), mod_c[0], mod_c[1], g1,
                  w_in_b[:, aw:aw + bn], kg, bd, kvw=kvw)
    attn = _attn(q, kv, kvc, _window_bias(seq), sinks[0].astype(F32),
                 batch=batch, seq=seq, ctx_len=ctx_len, nkv=nkv, grp=grp, kvw=kvw)
    pool = _pool(u, w_pool[0].astype(BF16), pool_scale, seq=seq)
    x1 = _outproj(attn, pool, w_out_b, x2, mod_b[:, 2], seq=seq)

    bm5 = min(512, seq)
    tri = (jnp.arange(bm5)[:, None] < jnp.arange(bm5)[None, :]).astype(BF16)
    hp, meta_i, meta_w, cnt = _router(x1, mod_b[:, 3], mod_b[:, 4], norm2_g[0][None, :],
                                      w_router[0].T.astype(BF16), b_router[0][:, None], tri, seq=seq)
    counts = cnt[:, 0].astype(I32)
    bmoe = 512 if m * TOP_K >= 1024 * n_exp else 128
    nck_up, nck_dn = 4, 2
    n_tiles = (m * TOP_K) // bmoe + n_exp
    items_up, row_start = _work_items(counts, bmoe=bmoe, nck=nck_up, n_tiles=n_tiles)
    items_dn, _ = _work_items(counts, bmoe=bmoe, nck=nck_dn, n_tiles=n_tiles)
    idx, rank = meta_i[:TOP_K], meta_i[TOP_K:]
    start_of = jnp.sum(jnp.where(idx[..., None] == jnp.arange(n_exp), row_start, 0), axis=-1)
    pos = (start_of + rank).astype(I32)

    bd_tok = min(256, seq)
    pos_d = pos.reshape(TOP_K, m // bd_tok, bd_tok).transpose(1, 0, 2)
    xs = _dispatch(pos_d, hp, rows=n_tiles * bmoe)
    act = _moe_up(items_up, xs, w_gate[0], w_up[0], b_gate[0][:, None, :], b_up[0][:, None, :],
                  bmoe=bmoe, nck=nck_up)
    ys = _moe_down(items_dn, act, w_down[0], b_down[0][:, None, :], bmoe=bmoe, nck=nck_dn)
    bc_tok = min(128, seq)
    pos_c = pos.reshape(TOP_K, m // bc_tok, bc_tok).transpose(1, 0, 2)
    out = _combine(pos_c, meta_w.T, x1, mod_b[:, 5], ys, seq=seq, nck=nck_dn)
    return out.reshape(batch, seq, d)
```

```python
import functools

import jax
import jax.numpy as jnp
from jax import lax
from jax.experimental import pallas as pl
from jax.experimental.pallas import tpu as pltpu

F32 = jnp.float32
BF16 = jnp.bfloat16
U32 = jnp.uint32
I32 = jnp.int32

HEAD_DIM = 64
LANES = 128
GRID_W = 64
WINDOW = 128
BLOCK = 128
ROPE_HALF = HEAD_DIM // 2
ROPE_BASE = 10000.0
POOL_WINDOWS = (2, 4, 8, 16)
POOL_HALO = 8
TOP_K = 4
N_MOD = 6
NORM_EPS = 1e-6
NEG_INF = -1e30
SWIGLU_LIMIT = 7.0
SWIGLU_ALPHA = 1.702
VMEM_LIMIT = 56 * 1024 * 1024


def _cparams(sem):
    return pltpu.CompilerParams(dimension_semantics=sem, vmem_limit_bytes=VMEM_LIMIT)


def _pack_pair(lo, hi):
    a = pltpu.bitcast(lo.astype(BF16).astype(F32), U32) >> 16
    b = pltpu.bitcast(hi.astype(BF16).astype(F32), U32) & jnp.uint32(0xFFFF0000)
    return a | b


def _unpack_pair(w):
    lo = pltpu.bitcast(w << 16, F32)
    hi = pltpu.bitcast(w & jnp.uint32(0xFFFF0000), F32)
    return lo, hi


def _ada_kernel(c_ref, w_ref, b_ref, o_ref):
    c = c_ref[...]
    s = c * jax.nn.sigmoid(c)
    hi = s.astype(BF16)
    lo = (s - hi.astype(F32)).astype(BF16)
    lhs = jnp.concatenate([hi, lo], axis=0)
    r = jnp.dot(lhs, w_ref[...].astype(BF16), preferred_element_type=F32)
    o_ref[...] = r[:8] + r[8:] + b_ref[...]


def _ada(c8, w_ada, b_ada):
    d, n = w_ada.shape
    bn = min(512, n)
    return pl.pallas_call(
        _ada_kernel,
        grid=(n // bn,),
        in_specs=[pl.BlockSpec((8, d), lambda j: (0, 0)),
                  pl.BlockSpec((d, bn), lambda j: (0, j)),
                  pl.BlockSpec((1, bn), lambda j: (0, j))],
        out_specs=pl.BlockSpec((8, bn), lambda j: (0, j)),
        out_shape=jax.ShapeDtypeStruct((8, n), F32),
        compiler_params=_cparams(("parallel",)),
        name="ada",
    )(c8, w_ada, b_ada)


def _norm_mod(x, g, shift, scale):
    ms = jnp.mean(x * x, axis=-1, keepdims=True)
    y = x * lax.rsqrt(ms + NORM_EPS) * g
    return y * (1.0 + scale) + shift


def _head_norm(z, bd, gain):
    ssq = jnp.dot((z * z).astype(BF16), bd, preferred_element_type=F32)
    return z * lax.rsqrt(ssq * (1.0 / HEAD_DIM) + NORM_EPS) * gain


def _rope(z, cos, sin):
    rows = z.shape[0]
    lane = lax.broadcasted_iota(I32, (rows, LANES), 1)
    first = (lane % ROPE_HALF) < (ROPE_HALF // 2)
    outs = []
    for s in range(z.shape[1] // LANES):
        zs = z[:, s * LANES:(s + 1) * LANES]
        up = pltpu.roll(zs, LANES - ROPE_HALF // 2, 1)
        dn = pltpu.roll(zs, ROPE_HALF // 2, 1)
        outs.append(zs * cos + jnp.where(first, up, dn) * sin)
    return outs[0] if len(outs) == 1 else jnp.concatenate(outs, axis=1)


def _tile_gain(g_ref, width):
    g = g_ref[...]
    return g if width == LANES else jnp.concatenate([g] * (width // LANES), axis=1)


def _inproj_kernel(x_ref, sh_ref, sc_ref, g1_ref, w_ref, qg_ref, kg_ref, cos_ref, sin_ref, bd_ref,
                   q_ref, kv_ref, u_ref, h_ref, *, nq_t, kvw):
    j = pl.program_id(1)

    @pl.when(j == 0)
    def _():
        h_ref[...] = _norm_mod(x_ref[...], g1_ref[...], sh_ref[...], sc_ref[...]).astype(BF16)

    z = jnp.dot(h_ref[...], w_ref[...], preferred_element_type=F32)
    bn = z.shape[1]

    @pl.when(j < nq_t)
    def _():
        zn = _head_norm(z, bd_ref[...], _tile_gain(qg_ref, bn))
        q_ref[...] = (_rope(zn, cos_ref[...], sin_ref[...]) * (HEAD_DIM ** -0.5)).astype(BF16)

    @pl.when(j == nq_t)
    def _():
        zk = z[:, :kvw]
        kn = _head_norm(zk, bd_ref[:kvw, :kvw], _tile_gain(kg_ref, kvw))
        kv_ref[:, :kvw] = _rope(kn, cos_ref[...], sin_ref[...]).astype(BF16)
        kv_ref[:, kvw:] = z[:, kvw:].astype(BF16)

    @pl.when(j > nq_t)
    def _():
        u_ref[...] = z


def _inproj(x2, shift, scale, g1, w_in_b, qg, kg, cos_t, sin_t, bd, *, seq, aw, kvw, pw):
    m, d = x2.shape
    bn = 2 * kvw
    nq_t, nu_t = aw // bn, pw // bn
    bm = min(512, seq)
    tpb = seq // bm
    kern = functools.partial(_inproj_kernel, nq_t=nq_t, kvw=kvw)
    return pl.pallas_call(
        kern,
        grid=(m // bm, nq_t + 1 + nu_t),
        in_specs=[pl.BlockSpec((bm, d), lambda i, j: (i, 0)),
                  pl.BlockSpec((None, 1, d), lambda i, j: (i // tpb, 0, 0)),
                  pl.BlockSpec((None, 1, d), lambda i, j: (i // tpb, 0, 0)),
                  pl.BlockSpec((1, d), lambda i, j: (0, 0)),
                  pl.BlockSpec((d, bn), lambda i, j: (0, j)),
                  pl.BlockSpec((1, LANES), lambda i, j: (0, 0)),
                  pl.BlockSpec((1, LANES), lambda i, j: (0, 0)),
                  pl.BlockSpec((bm, LANES), lambda i, j: (i % tpb, 0)),
                  pl.BlockSpec((bm, LANES), lambda i, j: (i % tpb, 0)),
                  pl.BlockSpec((bn, bn), lambda i, j: (0, 0))],
        out_specs=[pl.BlockSpec((bm, bn), lambda i, j: (i, jnp.minimum(j, nq_t - 1))),
                   pl.BlockSpec((bm, bn), lambda i, j: (i, 0)),
                   pl.BlockSpec((bm, bn), lambda i, j: (i, jnp.clip(j - nq_t - 1, 0, nu_t - 1)))],
        out_shape=[jax.ShapeDtypeStruct((m, aw), BF16),
                   jax.ShapeDtypeStruct((m, 2 * kvw), BF16),
                   jax.ShapeDtypeStruct((m, pw), F32)],
        scratch_shapes=[pltpu.VMEM((bm, d), BF16)],
        compiler_params=_cparams(("parallel", "arbitrary")),
        name="inproj",
    )(x2, shift, scale, g1, w_in_b, qg, kg, cos_t, sin_t, bd)


def _ctx_kv_kernel(x_ref, sh_ref, sc_ref, g1_ref, w_ref, kg_ref, bd_ref, kv_ref, *, kvw):
    h = _norm_mod(x_ref[...], g1_ref[...], sh_ref[...], sc_ref[...]).astype(BF16)
    z = jnp.dot(h, w_ref[...], preferred_element_type=F32)
    kn = _head_norm(z[:, :kvw], bd_ref[:kvw, :kvw], _tile_gain(kg_ref, kvw))
    kv_ref[:, :kvw] = kn.astype(BF16)
    kv_ref[:, kvw:] = z[:, kvw:].astype(BF16)


def _ctx_kv(c2, shift, scale, g1, w_kv_b, kg, bd, *, kvw):
    m, d = c2.shape
    bn = 2 * kvw
    bm = min(256, m)
    kern = functools.partial(_ctx_kv_kernel, kvw=kvw)
    return pl.pallas_call(
        kern,
        grid=(m // bm,),
        in_specs=[pl.BlockSpec((bm, d), lambda i: (i, 0)),
                  pl.BlockSpec((1, d), lambda i: (0, 0)),
                  pl.BlockSpec((1, d), lambda i: (0, 0)),
                  pl.BlockSpec((1, d), lambda i: (0, 0)),
                  pl.BlockSpec((d, bn), lambda i: (0, 0)),
                  pl.BlockSpec((1, LANES), lambda i: (0, 0)),
                  pl.BlockSpec((bn, bn), lambda i: (0, 0))],
        out_specs=pl.BlockSpec((bm, bn), lambda i: (i, 0)),
        out_shape=jax.ShapeDtypeStruct((m, bn), BF16),
        compiler_params=_cparams(("parallel",)),
        name="ctx_kv",
    )(c2, shift, scale, g1, w_kv_b, kg, bd)


def _attn_kernel(sink_ref, q_ref, kp_ref, kc_ref, kn_ref, kx_ref, bias_ref, o_ref, *, nkv, grp, kvw, ctx_len):
    npair = grp // 2
    nk = ctx_len + 3 * BLOCK
    lane_kv = lax.broadcasted_iota(I32, (nk, LANES), 1)
    lane_q = lax.broadcasted_iota(I32, (BLOCK, LANES), 1)
    bias = bias_ref[...]
    for h in range(nkv):
        slab = (h // 2) * LANES
        mine = (lane_kv // HEAD_DIM) == (h % 2)

        def window(off):
            w = jnp.concatenate([kx_ref[:, off + slab:off + slab + LANES],
                                 kp_ref[:, off + slab:off + slab + LANES],
                                 kc_ref[:, off + slab:off + slab + LANES],
                                 kn_ref[:, off + slab:off + slab + LANES]], axis=0).astype(F32)
            wz = jnp.where(mine, w, 0.0)
            wr = pltpu.roll(wz, HEAD_DIM, 1)
            lo, hi = (wz, wr) if h % 2 == 0 else (wr, wz)
            return lo.astype(BF16), hi.astype(BF16)

        k_e, k_o = window(0)
        v_e, v_o = window(kvw)
        q0 = h * grp * HEAD_DIM
        q4 = jnp.concatenate([q_ref[:, q0 + p * LANES:q0 + (p + 1) * LANES] for p in range(npair)], axis=0)
        kcat = jnp.concatenate([k_e, k_o], axis=0)
        s_all = lax.dot_general(q4, kcat, (((1,), (1,)), ((), ())), preferred_element_type=F32)
        p_e, p_o, rinv = [], [], []
        for p in range(npair):
            halves = []
            for half in range(2):
                sink = sink_ref[h * grp + 2 * p + half]
                s = s_all[p * BLOCK:(p + 1) * BLOCK, half * nk:(half + 1) * nk]
                s_ctx = s[:, :ctx_len]
                s_loc = s[:, ctx_len:] + bias
                mx = jnp.maximum(jnp.maximum(jnp.max(s_ctx, axis=-1, keepdims=True),
                                             jnp.max(s_loc, axis=-1, keepdims=True)), sink)
                e_ctx = jnp.exp(s_ctx - mx)
                e_loc = jnp.exp(s_loc - mx)
                den = (jnp.sum(e_ctx, axis=-1, keepdims=True) + jnp.sum(e_loc, axis=-1, keepdims=True)
                       + jnp.exp(sink - mx))
                halves.append((jnp.concatenate([e_ctx, e_loc], axis=1).astype(BF16), 1.0 / den))
            p_e.append(halves[0][0])
            p_o.append(halves[1][0])
            rinv.append(jnp.where(lane_q < HEAD_DIM, halves[0][1], halves[1][1]))
        o = (jnp.dot(jnp.concatenate(p_e, axis=0), v_e, preferred_element_type=F32)
             + jnp.dot(jnp.concatenate(p_o, axis=0), v_o, preferred_element_type=F32))
        for p in range(npair):
            o_ref[:, q0 + p * LANES:q0 + (p + 1) * LANES] = (o[p * BLOCK:(p + 1) * BLOCK] * rinv[p]).astype(BF16)


def _attn(q, kv, kvc, bias, sinks, *, batch, seq, ctx_len, nkv, grp, kvw):
    m, aw = q.shape
    nb = seq // BLOCK
    kern = functools.partial(_attn_kernel, nkv=nkv, grp=grp, kvw=kvw, ctx_len=ctx_len)

    def sel(n):
        first = 1 - jnp.minimum(n, 1)
        last = 1 - jnp.minimum(nb - 1 - n, 1)
        return 1 - first + last

    return pl.pallas_call(
        kern,
        grid=(batch, nb),
        in_specs=[pl.BlockSpec(memory_space=pltpu.SMEM),
                  pl.BlockSpec((BLOCK, aw), lambda b, n: (b * nb + n, 0)),
                  pl.BlockSpec((BLOCK, 2 * kvw), lambda b, n: (b * nb + jnp.maximum(n - 1, 0), 0)),
                  pl.BlockSpec((BLOCK, 2 * kvw), lambda b, n: (b * nb + n, 0)),
                  pl.BlockSpec((BLOCK, 2 * kvw), lambda b, n: (b * nb + jnp.minimum(n + 1, nb - 1), 0)),
                  pl.BlockSpec((ctx_len, 2 * kvw), lambda b, n: (b, 0)),
                  pl.BlockSpec((None, BLOCK, 3 * BLOCK), lambda b, n: (sel(n), 0, 0))],
        out_specs=pl.BlockSpec((BLOCK, aw), lambda b, n: (b * nb + n, 0)),
        out_shape=jax.ShapeDtypeStruct((m, aw), BF16),
        compiler_params=_cparams(("parallel", "parallel")),
        name="attn",
    )(sinks, q, kv, kv, kv, kvc, bias)


def _pool_kernel(up_ref, uc_ref, un_ref, wp_ref, ps_ref, o_ref, *, seq, pgw):
    bm = uc_ref.shape[0]
    t0 = (pl.program_id(0) * bm) % seq
    rows = bm + 2 * POOL_HALO
    pt = t0 + lax.broadcasted_iota(I32, (bm, rows), 0)
    ps = t0 - POOL_HALO + lax.broadcasted_iota(I32, (bm, rows), 1)
    pt1 = t0 + lax.broadcasted_iota(I32, (bm, 1), 0)
    for g, w in enumerate(POOL_WINDOWS):
        cs = slice(g * pgw, (g + 1) * pgw)
        uc = uc_ref[:, cs]
        ucat = jnp.concatenate([up_ref[:, cs], uc, un_ref[:, cs]], axis=0)
        u_hi = ucat.astype(BF16)
        u_lo = (ucat - u_hi.astype(F32)).astype(BF16)
        lo = jnp.clip(pt - w // 2, 0, seq)
        hi = jnp.clip(pt + w // 2, 0, seq)
        band = jnp.where((ps >= lo) & (ps < hi), 1.0, 0.0).astype(BF16)
        cnt = (jnp.clip(pt1 + w // 2, 0, seq) - jnp.clip(pt1 - w // 2, 0, seq)).astype(F32)
        tot = (jnp.dot(band, u_hi, preferred_element_type=F32)
               + jnp.dot(band, u_lo, preferred_element_type=F32))
        dlt = tot / cnt - uc
        y = jnp.dot(dlt.astype(BF16), wp_ref[g], preferred_element_type=F32)
        o_ref[:, cs] = (y * ps_ref[:, cs]).astype(BF16)


def _pool(u, w_pool_b, pool_scale, *, seq):
    m, pw = u.shape
    pgw = pw // len(POOL_WINDOWS)
    bm = min(256, seq)
    hb = bm // POOL_HALO
    nh = m // POOL_HALO
    kern = functools.partial(_pool_kernel, seq=seq, pgw=pgw)
    return pl.pallas_call(
        kern,
        grid=(m // bm,),
        in_specs=[pl.BlockSpec((POOL_HALO, pw), lambda i: (jnp.maximum(i * hb - 1, 0), 0)),
                  pl.BlockSpec((bm, pw), lambda i: (i, 0)),
                  pl.BlockSpec((POOL_HALO, pw), lambda i: (jnp.minimum((i + 1) * hb, nh - 1), 0)),
                  pl.BlockSpec((len(POOL_WINDOWS), pgw, pgw), lambda i: (0, 0, 0)),
                  pl.BlockSpec((1, pw), lambda i: (0, 0))],
        out_specs=pl.BlockSpec((bm, pw), lambda i: (i, 0)),
        out_shape=jax.ShapeDtypeStruct((m, pw), BF16),
        compiler_params=_cparams(("parallel",)),
        name="pool",
    )(u, u, u, w_pool_b, pool_scale)


def _outproj_kernel(a_ref, p_ref, wa_ref, wp_ref, x_ref, g_ref, o_ref):
    mix = (jnp.dot(a_ref[...], wa_ref[...], preferred_element_type=F32)
           + jnp.dot(p_ref[...], wp_ref[...], preferred_element_type=F32))
    o_ref[...] = x_ref[...] + g_ref[...] * mix


def _outproj(attn, pool, w_out_b, x2, gate, *, seq):
    m, d = x2.shape
    aw = attn.shape[1]
    bm = min(1024, seq)
    bn = min(512, d)
    tpb = seq // bm
    return pl.pallas_call(
        _outproj_kernel,
        grid=(m // bm, d // bn),
        in_specs=[pl.BlockSpec((bm, aw), lambda i, j: (i, 0)),
                  pl.BlockSpec((bm, aw), lambda i, j: (i, 0)),
                  pl.BlockSpec((aw, bn), lambda i, j: (0, j)),
                  pl.BlockSpec((aw, bn), lambda i, j: (1, j)),
                  pl.BlockSpec((bm, bn), lambda i, j: (i, j)),
                  pl.BlockSpec((None, 1, bn), lambda i, j: (i // tpb, 0, j))],
        out_specs=pl.BlockSpec((bm, bn), lambda i, j: (i, j)),
        out_shape=jax.ShapeDtypeStruct((m, d), F32),
        compiler_params=_cparams(("parallel", "parallel")),
        name="outproj",
    )(attn, pool, w_out_b, w_out_b, x2, gate)


def _router_kernel(x_ref, sh_ref, sc_ref, g2_ref, wr_ref, br_ref, tri_ref,
                   hp_ref, mi_ref, mw_ref, cnt_ref, carry_ref, *, n_exp):
    i = pl.program_id(0)

    @pl.when(i == 0)
    def _():
        carry_ref[...] = jnp.zeros_like(carry_ref)

    h = _norm_mod(x_ref[...], g2_ref[...], sh_ref[...], sc_ref[...])
    bm, d = h.shape
    hp_ref[...] = _pack_pair(h[:, :d // 2], h[:, d // 2:])
    logits = lax.dot_general(wr_ref[...], h.astype(BF16), (((1,), (1,)), ((), ())),
                             preferred_element_type=F32) + br_ref[...]
    eidx = lax.broadcasted_iota(I32, (n_exp, bm), 0)
    l = logits
    tops, idxs, sels = [], [], []
    for _ in range(TOP_K):
        mx = jnp.max(l, axis=0, keepdims=True)
        ix = jnp.min(jnp.where(l == mx, eidx, n_exp), axis=0, keepdims=True)
        sel = eidx == ix
        tops.append(mx)
        idxs.append(ix)
        sels.append(sel)
        l = jnp.where(sel, -jnp.inf, l)
    es = [jnp.exp(t - tops[0]) for t in tops]
    den = es[0] + es[1] + es[2] + es[3]
    member = jnp.where(sels[0] | sels[1] | sels[2] | sels[3], 1.0, 0.0)
    before = jnp.dot(member.astype(BF16), tri_ref[...], preferred_element_type=F32) + carry_ref[:, 0:1]
    ranks = [jnp.sum(jnp.where(s, before, 0.0), axis=0, keepdims=True).astype(I32) for s in sels]
    carry_ref[...] = carry_ref[...] + jnp.sum(member, axis=1, keepdims=True)
    mi_ref[...] = jnp.concatenate(idxs + ranks, axis=0)
    mw_ref[...] = jnp.concatenate([e / den for e in es] + [jnp.zeros((TOP_K, bm), F32)], axis=0)
    cnt_ref[...] = carry_ref[...]


def _router(x1, shift, scale, g2, w_rt_b, b_r, tri, *, seq):
    m, d = x1.shape
    n_exp = w_rt_b.shape[0]
    bm = tri.shape[0]
    tpb = seq // bm
    kern = functools.partial(_router_kernel, n_exp=n_exp)
    return pl.pallas_call(
        kern,
        grid=(m // bm,),
        in_specs=[pl.BlockSpec((bm, d), lambda i: (i, 0)),
                  pl.BlockSpec((None, 1, d), lambda i: (i // tpb, 0, 0)),
                  pl.BlockSpec((None, 1, d), lambda i: (i // tpb, 0, 0)),
                  pl.BlockSpec((1, d), lambda i: (0, 0)),
                  pl.BlockSpec((n_exp, d), lambda i: (0, 0)),
                  pl.BlockSpec((n_exp, 1), lambda i: (0, 0)),
                  pl.BlockSpec((bm, bm), lambda i: (0, 0))],
        out_specs=[pl.BlockSpec((bm, d // 2), lambda i: (i, 0)),
                   pl.BlockSpec((2 * TOP_K, bm), lambda i: (0, i)),
                   pl.BlockSpec((2 * TOP_K, bm), lambda i: (0, i)),
                   pl.BlockSpec((n_exp, LANES), lambda i: (0, 0))],
        out_shape=[jax.ShapeDtypeStruct((m, d // 2), U32),
                   jax.ShapeDtypeStruct((2 * TOP_K, m), I32),
                   jax.ShapeDtypeStruct((2 * TOP_K, m), F32),
                   jax.ShapeDtypeStruct((n_exp, LANES), F32)],
        scratch_shapes=[pltpu.VMEM((n_exp, LANES), F32)],
        compiler_params=_cparams(("arbitrary",)),
        name="router",
    )(x1, shift, scale, g2, w_rt_b, b_r, tri)


def _row_copy(src, s_row, dst, d_row, sem):
    return pltpu.make_async_copy(src.at[pl.ds(s_row, 1)], dst.at[pl.ds(d_row, 1)], sem)


ISSUE_UNROLL = 8


def _dispatch_kernel(pos_ref, h_ref, xs_ref, pos_s, sem_p, sem):
    bd = h_ref.shape[0]
    cp = pltpu.make_async_copy(pos_ref, pos_s, sem_p)
    cp.start()
    cp.wait()

    def issue(tb, c):
        for k in range(ISSUE_UNROLL):
            t = tb * ISSUE_UNROLL + k
            for s in range(TOP_K):
                _row_copy(h_ref, t, xs_ref, pos_s[s, t], sem).start()
        return c

    lax.fori_loop(0, bd // ISSUE_UNROLL, issue, 0)
    for s in range(TOP_K):
        pltpu.make_async_copy(h_ref, xs_ref.at[pl.ds(0, bd)], sem).wait()


def _dispatch(pos8, hp, *, rows, bd):
    m, dh = hp.shape
    return pl.pallas_call(
        _dispatch_kernel,
        grid=(m // bd,),
        in_specs=[pl.BlockSpec((2 * TOP_K, bd), lambda i: (0, i)),
                  pl.BlockSpec((bd, dh), lambda i: (i, 0))],
        out_specs=pl.BlockSpec(memory_space=pl.ANY),
        out_shape=jax.ShapeDtypeStruct((rows, dh), U32),
        scratch_shapes=[pltpu.SMEM((2 * TOP_K, bd), I32), pltpu.SemaphoreType.DMA(()),
                        pltpu.SemaphoreType.DMA(())],
        compiler_params=_cparams(("arbitrary",)),
        name="dispatch",
    )(pos8, hp)


def _moe_up_kernel(ie_ref, ic_ref, it_ref, if_ref, iv_ref, in_ref,
                   xs_ref, wg_ref, wu_ref, bg_ref, bu_ref, a_ref, wg_s, wu_s):
    l = pl.program_id(0)

    @pl.when(if_ref[l] == 1)
    def _():
        wg_s[...] = wg_ref[...].astype(BF16)
        wu_s[...] = wu_ref[...].astype(BF16)

    @pl.when(iv_ref[l] == 1)
    def _():
        w = xs_ref[...]
        row = lax.broadcasted_iota(I32, w.shape, 0)
        w = jnp.where(row < in_ref[l], w, jnp.uint32(0))
        lo, hi = _unpack_pair(w)
        x = jnp.concatenate([lo.astype(BF16), hi.astype(BF16)], axis=1)
        gate = jnp.minimum(jnp.dot(x, wg_s[...], preferred_element_type=F32) + bg_ref[...], SWIGLU_LIMIT)
        lin = jnp.clip(jnp.dot(x, wu_s[...], preferred_element_type=F32) + bu_ref[...],
                       -SWIGLU_LIMIT, SWIGLU_LIMIT)
        a_ref[...] = (gate * jax.nn.sigmoid(SWIGLU_ALPHA * gate) * (lin + 1.0)).astype(BF16)


def _moe_up(items, xs, w_gate, w_up, b_gate, b_up, *, bmoe, nck):
    rows, dh = xs.shape
    n_exp, d, ff = w_gate.shape
    fc = ff // nck
    n_items = items[0].shape[0]
    grid_spec = pltpu.PrefetchScalarGridSpec(
        num_scalar_prefetch=6,
        grid=(n_items,),
        in_specs=[pl.BlockSpec((bmoe, dh), lambda l, ie, ic, it, *_: (it[l], 0)),
                  pl.BlockSpec((None, d, fc), lambda l, ie, ic, it, *_: (ie[l], 0, ic[l])),
                  pl.BlockSpec((None, d, fc), lambda l, ie, ic, it, *_: (ie[l], 0, ic[l])),
                  pl.BlockSpec((None, 1, fc), lambda l, ie, ic, it, *_: (ie[l], 0, ic[l])),
                  pl.BlockSpec((None, 1, fc), lambda l, ie, ic, it, *_: (ie[l], 0, ic[l]))],
        out_specs=pl.BlockSpec((bmoe, fc), lambda l, ie, ic, it, *_: (it[l], ic[l])),
        scratch_shapes=[pltpu.VMEM((d, fc), BF16), pltpu.VMEM((d, fc), BF16)],
    )
    return pl.pallas_call(
        _moe_up_kernel,
        grid_spec=grid_spec,
        out_shape=jax.ShapeDtypeStruct((rows, ff), BF16),
        compiler_params=_cparams(("arbitrary",)),
        name="moe_up",
    )(*items, xs, w_gate, w_up, b_gate, b_up)


def _moe_down_kernel(ie_ref, ic_ref, it_ref, if_ref, iv_ref, in_ref, a_ref, wd_ref, bd_ref, y_ref, wd_s):
    l = pl.program_id(0)

    @pl.when(if_ref[l] == 1)
    def _():
        wd_s[...] = wd_ref[...].astype(BF16)

    @pl.when(iv_ref[l] == 1)
    def _():
        y = jnp.dot(a_ref[...], wd_s[...], preferred_element_type=F32) + bd_ref[...]
        half = y.shape[1] // 2
        y_ref[...] = _pack_pair(y[:, :half], y[:, half:])


def _moe_down(items, act, w_down, b_down, *, bmoe, nck):
    rows, ff = act.shape
    n_exp, _, d = w_down.shape
    dc = d // nck
    n_items = items[0].shape[0]
    grid_spec = pltpu.PrefetchScalarGridSpec(
        num_scalar_prefetch=6,
        grid=(n_items,),
        in_specs=[pl.BlockSpec((bmoe, ff), lambda l, ie, ic, it, *_: (it[l], 0)),
                  pl.BlockSpec((None, ff, dc), lambda l, ie, ic, it, *_: (ie[l], 0, ic[l])),
                  pl.BlockSpec((None, 1, dc), lambda l, ie, ic, it, *_: (ie[l], 0, ic[l]))],
        out_specs=pl.BlockSpec((bmoe, dc // 2), lambda l, ie, ic, it, *_: (it[l], ic[l])),
        scratch_shapes=[pltpu.VMEM((ff, dc), BF16)],
    )
    return pl.pallas_call(
        _moe_down_kernel,
        grid_spec=grid_spec,
        out_shape=jax.ShapeDtypeStruct((rows, d // 2), U32),
        compiler_params=_cparams(("arbitrary",)),
        name="moe_down",
    )(*items, act, w_down, b_down)


def _combine_kernel(pos_ref, posn_ref, w_ref, x_ref, g_ref, ys_ref, o_ref, pos_s, buf, sem_p, sem, *, nck):
    i = pl.program_id(0)
    bc = x_ref.shape[0]
    slot = i % 2

    def gather(p_ref, sl):
        cp = pltpu.make_async_copy(p_ref, pos_s, sem_p)
        cp.start()
        cp.wait()

        def issue(tb, c):
            for k in range(ISSUE_UNROLL):
                t = tb * ISSUE_UNROLL + k
                for s in range(TOP_K):
                    _row_copy(ys_ref, pos_s[s, t], buf.at[sl, s], t, sem.at[sl]).start()
            return c

        lax.fori_loop(0, bc // ISSUE_UNROLL, issue, 0)

    @pl.when(i == 0)
    def _():
        gather(pos_ref, 0)

    @pl.when(i + 1 < pl.num_programs(0))
    def _():
        gather(posn_ref, 1 - slot)

    for s in range(TOP_K):
        pltpu.make_async_copy(ys_ref.at[pl.ds(0, bc)], buf.at[slot, s], sem.at[slot]).wait()

    dq = x_ref.shape[1] // (2 * nck)
    acc = [jnp.zeros((bc, dq), F32) for _ in range(2 * nck)]
    for s in range(TOP_K):
        ws = w_ref[:, s:s + 1]
        for c in range(nck):
            lo, hi = _unpack_pair(buf[slot, s, :, c * dq:(c + 1) * dq])
            acc[2 * c] = acc[2 * c] + ws * lo
            acc[2 * c + 1] = acc[2 * c + 1] + ws * hi
    for k in range(2 * nck):
        cs = slice(k * dq, (k + 1) * dq)
        o_ref[:, cs] = x_ref[:, cs] + g_ref[:, cs] * acc[k]


def _combine(pos8, wtok, x1, gate, ys, *, seq, nck, bc):
    m, d = x1.shape
    tpb = seq // bc
    nsteps = m // bc
    kern = functools.partial(_combine_kernel, nck=nck)
    return pl.pallas_call(
        kern,
        grid=(nsteps,),
        in_specs=[pl.BlockSpec((2 * TOP_K, bc), lambda i: (0, i)),
                  pl.BlockSpec((2 * TOP_K, bc), lambda i: (0, jnp.minimum(i + 1, nsteps - 1))),
                  pl.BlockSpec((bc, 2 * TOP_K), lambda i: (i, 0)),
                  pl.BlockSpec((bc, d), lambda i: (i, 0)),
                  pl.BlockSpec((None, 1, d), lambda i: (i // tpb, 0, 0)),
                  pl.BlockSpec(memory_space=pl.ANY)],
        out_specs=pl.BlockSpec((bc, d), lambda i: (i, 0)),
        out_shape=jax.ShapeDtypeStruct((m, d), F32),
        scratch_shapes=[pltpu.SMEM((2 * TOP_K, bc), I32), pltpu.VMEM((2, TOP_K, bc, d // 2), U32),
                        pltpu.SemaphoreType.DMA(()), pltpu.SemaphoreType.DMA((2,))],
        compiler_params=_cparams(("arbitrary",)),
        name="combine",
    )(pos8, pos8, wtok, x1, gate, ys)


def _rope_tables(seq):
    t = jnp.arange(seq)
    row = (t // GRID_W).astype(F32)
    col = (t % GRID_W).astype(F32)
    inv_freq = ROPE_BASE ** (-jnp.arange(0, ROPE_HALF, 2, dtype=F32) / ROPE_HALF)
    ar = row[:, None] * inv_freq[None, :]
    ac = col[:, None] * inv_freq[None, :]
    cos = jnp.concatenate([jnp.cos(ar), jnp.cos(ar), jnp.cos(ac), jnp.cos(ac)], axis=1)
    sin = jnp.concatenate([-jnp.sin(ar), jnp.sin(ar), -jnp.sin(ac), jnp.sin(ac)], axis=1)
    rep = LANES // HEAD_DIM
    return jnp.tile(cos, (1, rep)), jnp.tile(sin, (1, rep))


def _window_bias(seq):
    i = jnp.arange(BLOCK)[:, None]
    j = jnp.arange(3 * BLOCK)[None, :]
    band = (j >= i) & (j <= i + 2 * WINDOW)
    first = band & (j >= BLOCK)
    last = band & (j < 2 * BLOCK)
    both = first & last
    variants = jnp.stack([both if seq == BLOCK else first, band, last])
    return jnp.where(variants, 0.0, NEG_INF).astype(F32)


def _work_items(counts, *, bmoe, nck, n_tiles):
    n_exp = counts.shape[0]
    tiles_e = (counts + bmoe - 1) // bmoe
    tile_end = jnp.cumsum(tiles_e)
    tile_start = tile_end - tiles_e
    item_end = nck * tile_end
    total = item_end[-1]
    l = jnp.arange(nck * n_tiles, dtype=I32)
    valid = l < total
    lc = jnp.minimum(l, total - 1)
    e = jnp.minimum(jnp.sum((item_end[None, :] <= lc[:, None]).astype(I32), axis=1), n_exp - 1)
    is_e = e[:, None] == jnp.arange(n_exp, dtype=I32)[None, :]
    of_e = lambda v: jnp.sum(jnp.where(is_e, v[None, :], 0), axis=1)
    nt = jnp.maximum(of_e(tiles_e), 1)
    ts = of_e(tile_start)
    within = lc - nck * ts
    c = within // nt
    k = within % nt
    t = ts + k
    first = (k == 0) & valid
    nrows = jnp.clip(of_e(counts) - k * bmoe, 0, bmoe)
    as_i = lambda a: a.astype(I32)
    return (as_i(e), as_i(c), as_i(t), as_i(first), as_i(valid), as_i(nrows)), tile_start * bmoe


def kernel(x, c, ctx, c_ctx, w_ada, b_ada, norm1_g, w_in, q_norm_g, k_norm_g, sinks, w_pool, pool_scale,
           w_out, norm2_g, w_router, b_router, w_gate, b_gate, w_up, b_up, w_down, b_down):
    assert w_ada.shape[0] == 1, "one layer: the context stream is only read, never updated"
    batch, seq, d = x.shape
    ctx_len = ctx.shape[1]
    aw = d // 2
    nq = aw // HEAD_DIM
    nkv = max(1, nq // 8)
    grp = nq // nkv
    kvw = nkv * HEAD_DIM
    pw = d - aw
    n_exp, _, ff = w_gate.shape[1:]
    m = batch * seq
    assert seq % BLOCK == 0 and seq % GRID_W == 0 and kvw % LANES == 0 and grp % 2 == 0 and aw == pw

    c8 = jnp.zeros((8, d), F32).at[:batch].set(c).at[batch].set(c_ctx)
    mod = _ada(c8, w_ada[0], b_ada)
    mod_b = mod[:batch].reshape(batch, N_MOD, 1, d)
    mod_c = mod[batch].reshape(N_MOD, 1, d)

    w_in_b = w_in[0].astype(BF16)
    w_out_b = w_out[0].astype(BF16)
    cos_t, sin_t = _rope_tables(seq)
    qg = jnp.tile(q_norm_g[0].astype(F32), LANES // HEAD_DIM)[None, :]
    kg = jnp.tile(k_norm_g[0].astype(F32), LANES // HEAD_DIM)[None, :]
    bn = 2 * kvw
    lane_head = jnp.arange(bn) // HEAD_DIM
    bd = (lane_head[:, None] == lane_head[None, :]).astype(BF16)
    g1 = norm1_g[0][None, :]

    x2 = x.reshape(m, d)
    q, kv, u = _inproj(x2, mod_b[:, 0], mod_b[:, 1], g1, w_in_b, qg, kg, cos_t, sin_t, bd,
                       seq=seq, aw=aw, kvw=kvw, pw=pw)
    kvc = _ctx_kv(ctx.reshape(batch * ctx_len, d), mod_c[0], mod_c[1], g1,
                  w_in_b[:, aw:aw + bn], kg, bd, kvw=kvw)
    attn = _attn(q, kv, kvc, _window_bias(seq), sinks[0].astype(F32),
                 batch=batch, seq=seq, ctx_len=ctx_len, nkv=nkv, grp=grp, kvw=kvw)
    pool = _pool(u, w_pool[0].astype(BF16), pool_scale, seq=seq)
    x1 = _outproj(attn, pool, w_out_b, x2, mod_b[:, 2], seq=seq)

    bm5 = min(512, seq)
    tri = (jnp.arange(bm5)[:, None] < jnp.arange(bm5)[None, :]).astype(BF16)
    hp, meta_i, meta_w, cnt = _router(x1, mod_b[:, 3], mod_b[:, 4], norm2_g[0][None, :],
                                      w_router[0].T.astype(BF16), b_router[0][:, None], tri, seq=seq)
    counts = cnt[:, 0].astype(I32)
    bmoe = 256 if m * TOP_K >= 512 * n_exp else 128
    nck_up, nck_dn = 2, 1
    n_tiles = (m * TOP_K) // bmoe + n_exp
    items_up, row_start = _work_items(counts, bmoe=bmoe, nck=nck_up, n_tiles=n_tiles)
    items_dn, _ = _work_items(counts, bmoe=bmoe, nck=nck_dn, n_tiles=n_tiles)
    idx, rank = meta_i[:TOP_K], meta_i[TOP_K:]
    start_of = jnp.sum(jnp.where(idx[..., None] == jnp.arange(n_exp), row_start, 0), axis=-1)
    pos8 = jnp.concatenate([start_of + rank, jnp.zeros_like(rank)], axis=0).astype(I32)

    xs = _dispatch(pos8, hp, rows=n_tiles * bmoe, bd=min(256, seq))
    act = _moe_up(items_up, xs, w_gate[0], w_up[0], b_gate[0][:, None, :], b_up[0][:, None, :],
                  bmoe=bmoe, nck=nck_up)
    ys = _moe_down(items_dn, act, w_down[0], b_down[0][:, None, :], bmoe=bmoe, nck=nck_dn)
    out = _combine(pos8, meta_w.T, x1, mod_b[:, 5], ys, seq=seq, nck=nck_dn, bc=min(256, seq))
    return out.reshape(batch, seq, d)
```

```python
import functools

import jax
import jax.numpy as jnp
from jax import lax
from jax.experimental import pallas as pl
from jax.experimental.pallas import tpu as pltpu

F32 = jnp.float32
BF16 = jnp.bfloat16
U32 = jnp.uint32
I32 = jnp.int32

HEAD_DIM = 64
LANES = 128
GRID_W = 64
WINDOW = 128
BLOCK = 128
ROPE_HALF = HEAD_DIM // 2
ROPE_BASE = 10000.0
POOL_WINDOWS = (2, 4, 8, 16)
POOL_HALO = 8
TOP_K = 4
N_MOD = 6
NORM_EPS = 1e-6
NEG_INF = -1e30
SWIGLU_LIMIT = 7.0
SWIGLU_ALPHA = 1.702
VMEM_LIMIT = 56 * 1024 * 1024


def _cparams(sem):
    return pltpu.CompilerParams(dimension_semantics=sem, vmem_limit_bytes=VMEM_LIMIT)


def _pack_pair(lo, hi):
    a = pltpu.bitcast(lo.astype(BF16).astype(F32), U32) >> 16
    b = pltpu.bitcast(hi.astype(BF16).astype(F32), U32) & jnp.uint32(0xFFFF0000)
    return a | b


def _unpack_pair(w):
    lo = pltpu.bitcast(w << 16, F32)
    hi = pltpu.bitcast(w & jnp.uint32(0xFFFF0000), F32)
    return lo, hi


def _ada_kernel(c_ref, w_ref, b_ref, o_ref):
    c = c_ref[...]
    s = c * jax.nn.sigmoid(c)
    hi = s.astype(BF16)
    lo = (s - hi.astype(F32)).astype(BF16)
    lhs = jnp.concatenate([hi, lo], axis=0)
    r = jnp.dot(lhs, w_ref[...].astype(BF16), preferred_element_type=F32)
    o_ref[...] = r[:8] + r[8:] + b_ref[...]


def _ada(c8, w_ada, b_ada):
    d, n = w_ada.shape
    bn = min(512, n)
    return pl.pallas_call(
        _ada_kernel,
        grid=(n // bn,),
        in_specs=[pl.BlockSpec((8, d), lambda j: (0, 0)),
                  pl.BlockSpec((d, bn), lambda j: (0, j)),
                  pl.BlockSpec((1, bn), lambda j: (0, j))],
        out_specs=pl.BlockSpec((8, bn), lambda j: (0, j)),
        out_shape=jax.ShapeDtypeStruct((8, n), F32),
        compiler_params=_cparams(("parallel",)),
        name="ada",
    )(c8, w_ada, b_ada)


def _norm_mod(x, g, shift, scale):
    ms = jnp.mean(x * x, axis=-1, keepdims=True)
    y = x * lax.rsqrt(ms + NORM_EPS) * g
    return y * (1.0 + scale) + shift


def _head_rnorm(z, bd):
    msq = jnp.dot((z * z).astype(BF16), bd, preferred_element_type=F32)
    return lax.rsqrt(msq + NORM_EPS)


def _rope(z, ta, tb):
    rows = z.shape[0]
    lane = lax.broadcasted_iota(I32, (rows, LANES), 1)
    first = (lane % ROPE_HALF) < (ROPE_HALF // 2)
    outs = []
    for s in range(z.shape[1] // LANES):
        zs = z[:, s * LANES:(s + 1) * LANES]
        up = pltpu.roll(zs, LANES - ROPE_HALF // 2, 1)
        dn = pltpu.roll(zs, ROPE_HALF // 2, 1)
        outs.append(zs * ta + jnp.where(first, up, dn) * tb)
    return outs[0] if len(outs) == 1 else jnp.concatenate(outs, axis=1)


def _tile_gain(g_ref, width):
    g = g_ref[...]
    return g if width == LANES else jnp.concatenate([g] * (width // LANES), axis=1)


def _inproj_kernel(x_ref, sh_ref, sc_ref, g1_ref, w_ref, qa_ref, qb_ref, ka_ref, kb_ref, bd_ref,
                   q_ref, kv_ref, u_ref, h_ref, zs_ref, *, nq_t, kvw):
    j = pl.program_id(1)
    cur = j % 2

    def mm():
        return jnp.dot(h_ref[...], w_ref[...], preferred_element_type=F32)

    @pl.when(j == 0)
    def _():
        h_ref[...] = _norm_mod(x_ref[...], g1_ref[...], sh_ref[...], sc_ref[...]).astype(BF16)
        zs_ref[0] = mm()

    @pl.when((j >= 1) & (j <= nq_t))
    def _():
        zp = zs_ref[1 - cur]
        zs_ref[cur] = mm()
        q_ref[...] = (_rope(zp, qa_ref[...], qb_ref[...]) * _head_rnorm(zp, bd_ref[...])).astype(BF16)

    @pl.when(j == nq_t + 1)
    def _():
        zp = zs_ref[1 - cur]
        u_ref[...] = mm()
        zk = zp[:, :kvw]
        kv_ref[:, :kvw] = (_rope(zk, ka_ref[...], kb_ref[...]) * _head_rnorm(zk, bd_ref[:kvw, :kvw])).astype(BF16)
        kv_ref[:, kvw:] = zp[:, kvw:].astype(BF16)

    @pl.when(j > nq_t + 1)
    def _():
        u_ref[...] = mm()


def _inproj(x2, shift, scale, g1, w_in_b, qa, qb, ka, kb, bd, *, seq, aw, kvw, pw):
    m, d = x2.shape
    bn = 2 * kvw
    nq_t, nu_t = aw // bn, pw // bn
    bm = min(512, seq)
    tpb = seq // bm
    kern = functools.partial(_inproj_kernel, nq_t=nq_t, kvw=kvw)
    return pl.pallas_call(
        kern,
        grid=(m // bm, nq_t + 1 + nu_t),
        in_specs=[pl.BlockSpec((bm, d), lambda i, j: (i, 0)),
                  pl.BlockSpec((None, 1, d), lambda i, j: (i // tpb, 0, 0)),
                  pl.BlockSpec((None, 1, d), lambda i, j: (i // tpb, 0, 0)),
                  pl.BlockSpec((1, d), lambda i, j: (0, 0)),
                  pl.BlockSpec((d, bn), lambda i, j: (0, j)),
                  pl.BlockSpec((bm, LANES), lambda i, j: (i % tpb, 0)),
                  pl.BlockSpec((bm, LANES), lambda i, j: (i % tpb, 0)),
                  pl.BlockSpec((bm, LANES), lambda i, j: (i % tpb, 0)),
                  pl.BlockSpec((bm, LANES), lambda i, j: (i % tpb, 0)),
                  pl.BlockSpec((bn, bn), lambda i, j: (0, 0))],
        out_specs=[pl.BlockSpec((bm, bn), lambda i, j: (i, jnp.clip(j - 1, 0, nq_t - 1))),
                   pl.BlockSpec((bm, bn), lambda i, j: (i, 0)),
                   pl.BlockSpec((bm, bn), lambda i, j: (i, jnp.clip(j - nq_t - 1, 0, nu_t - 1)))],
        out_shape=[jax.ShapeDtypeStruct((m, aw), BF16),
                   jax.ShapeDtypeStruct((m, 2 * kvw), BF16),
                   jax.ShapeDtypeStruct((m, pw), F32)],
        scratch_shapes=[pltpu.VMEM((bm, d), BF16), pltpu.VMEM((2, bm, bn), F32)],
        compiler_params=_cparams(("parallel", "arbitrary")),
        name="inproj",
    )(x2, shift, scale, g1, w_in_b, qa, qb, ka, kb, bd)


def _ctx_kv_kernel(x_ref, sh_ref, sc_ref, g1_ref, w_ref, kg_ref, bd_ref, kv_ref, *, kvw):
    h = _norm_mod(x_ref[...], g1_ref[...], sh_ref[...], sc_ref[...]).astype(BF16)
    z = jnp.dot(h, w_ref[...], preferred_element_type=F32)
    zk = z[:, :kvw]
    kv_ref[:, :kvw] = (zk * _head_rnorm(zk, bd_ref[:kvw, :kvw]) * _tile_gain(kg_ref, kvw)).astype(BF16)
    kv_ref[:, kvw:] = z[:, kvw:].astype(BF16)


def _ctx_kv(c2, shift, scale, g1, w_kv_b, kg, bd, *, kvw):
    m, d = c2.shape
    bn = 2 * kvw
    bm = min(256, m)
    kern = functools.partial(_ctx_kv_kernel, kvw=kvw)
    return pl.pallas_call(
        kern,
        grid=(m // bm,),
        in_specs=[pl.BlockSpec((bm, d), lambda i: (i, 0)),
                  pl.BlockSpec((1, d), lambda i: (0, 0)),
                  pl.BlockSpec((1, d), lambda i: (0, 0)),
                  pl.BlockSpec((1, d), lambda i: (0, 0)),
                  pl.BlockSpec((d, bn), lambda i: (0, 0)),
                  pl.BlockSpec((1, LANES), lambda i: (0, 0)),
                  pl.BlockSpec((bn, bn), lambda i: (0, 0))],
        out_specs=pl.BlockSpec((bm, bn), lambda i: (i, 0)),
        out_shape=jax.ShapeDtypeStruct((m, bn), BF16),
        compiler_params=_cparams(("parallel",)),
        name="ctx_kv",
    )(c2, shift, scale, g1, w_kv_b, kg, bd)


def _attn_kernel(sink_ref, q_ref, kp_ref, kc_ref, kn_ref, kx_ref, bias_ref, o_ref, *, nkv, grp, kvw, ctx_len):
    npair = grp // 2
    nk = ctx_len + 3 * BLOCK
    lane_kv = lax.broadcasted_iota(I32, (nk, LANES), 1)
    lane_q = lax.broadcasted_iota(I32, (BLOCK, LANES), 1)
    bias = bias_ref[...]
    for h in range(nkv):
        slab = (h // 2) * LANES
        mine = (lane_kv // HEAD_DIM) == (h % 2)

        def window(off):
            w = jnp.concatenate([kx_ref[:, off + slab:off + slab + LANES],
                                 kp_ref[:, off + slab:off + slab + LANES],
                                 kc_ref[:, off + slab:off + slab + LANES],
                                 kn_ref[:, off + slab:off + slab + LANES]], axis=0).astype(F32)
            wz = jnp.where(mine, w, 0.0)
            wr = pltpu.roll(wz, HEAD_DIM, 1)
            lo, hi = (wz, wr) if h % 2 == 0 else (wr, wz)
            return lo.astype(BF16), hi.astype(BF16)

        k_e, k_o = window(0)
        v_e, v_o = window(kvw)
        q0 = h * grp * HEAD_DIM
        q4 = jnp.concatenate([q_ref[:, q0 + p * LANES:q0 + (p + 1) * LANES] for p in range(npair)], axis=0)
        kcat = jnp.concatenate([k_e, k_o], axis=0)
        s_all = lax.dot_general(q4, kcat, (((1,), (1,)), ((), ())), preferred_element_type=F32)
        p_e, p_o, rinv = [], [], []
        for p in range(npair):
            halves = []
            for half in range(2):
                sink = sink_ref[h * grp + 2 * p + half]
                s = s_all[p * BLOCK:(p + 1) * BLOCK, half * nk:(half + 1) * nk]
                s_ctx = s[:, :ctx_len]
                s_loc = s[:, ctx_len:] + bias
                mx = jnp.maximum(jnp.maximum(jnp.max(s_ctx, axis=-1, keepdims=True),
                                             jnp.max(s_loc, axis=-1, keepdims=True)), sink)
                e_ctx = jnp.exp(s_ctx - mx)
                e_loc = jnp.exp(s_loc - mx)
                den = (jnp.sum(e_ctx, axis=-1, keepdims=True) + jnp.sum(e_loc, axis=-1, keepdims=True)
                       + jnp.exp(sink - mx))
                halves.append((jnp.concatenate([e_ctx, e_loc], axis=1).astype(BF16), 1.0 / den))
            p_e.append(halves[0][0])
            p_o.append(halves[1][0])
            rinv.append(jnp.where(lane_q < HEAD_DIM, halves[0][1], halves[1][1]))
        o = (jnp.dot(jnp.concatenate(p_e, axis=0), v_e, preferred_element_type=F32)
             + jnp.dot(jnp.concatenate(p_o, axis=0), v_o, preferred_element_type=F32))
        for p in range(npair):
            o_ref[:, q0 + p * LANES:q0 + (p + 1) * LANES] = (o[p * BLOCK:(p + 1) * BLOCK] * rinv[p]).astype(BF16)


def _attn(q, kv, kvc, bias, sinks, *, batch, seq, ctx_len, nkv, grp, kvw):
    m, aw = q.shape
    nb = seq // BLOCK
    kern = functools.partial(_attn_kernel, nkv=nkv, grp=grp, kvw=kvw, ctx_len=ctx_len)

    def sel(n):
        first = 1 - jnp.minimum(n, 1)
        last = 1 - jnp.minimum(nb - 1 - n, 1)
        return 1 - first + last

    return pl.pallas_call(
        kern,
        grid=(batch, nb),
        in_specs=[pl.BlockSpec(memory_space=pltpu.SMEM),
                  pl.BlockSpec((BLOCK, aw), lambda b, n: (b * nb + n, 0)),
                  pl.BlockSpec((BLOCK, 2 * kvw), lambda b, n: (b * nb + jnp.maximum(n - 1, 0), 0)),
                  pl.BlockSpec((BLOCK, 2 * kvw), lambda b, n: (b * nb + n, 0)),
                  pl.BlockSpec((BLOCK, 2 * kvw), lambda b, n: (b * nb + jnp.minimum(n + 1, nb - 1), 0)),
                  pl.BlockSpec((ctx_len, 2 * kvw), lambda b, n: (b, 0)),
                  pl.BlockSpec((None, BLOCK, 3 * BLOCK), lambda b, n: (sel(n), 0, 0))],
        out_specs=pl.BlockSpec((BLOCK, aw), lambda b, n: (b * nb + n, 0)),
        out_shape=jax.ShapeDtypeStruct((m, aw), BF16),
        compiler_params=_cparams(("parallel", "parallel")),
        name="attn",
    )(sinks, q, kv, kv, kv, kvc, bias)


def _pool_kernel(up_ref, uc_ref, un_ref, wp_ref, ps_ref, o_ref, *, seq, pgw):
    bm = uc_ref.shape[0]
    t0 = (pl.program_id(0) * bm) % seq
    rows = bm + 2 * POOL_HALO
    pt = t0 + lax.broadcasted_iota(I32, (bm, rows), 0)
    ps = t0 - POOL_HALO + lax.broadcasted_iota(I32, (bm, rows), 1)
    pt1 = t0 + lax.broadcasted_iota(I32, (bm, 1), 0)
    for g, w in enumerate(POOL_WINDOWS):
        cs = slice(g * pgw, (g + 1) * pgw)
        uc = uc_ref[:, cs]
        ucat = jnp.concatenate([up_ref[:, cs], uc, un_ref[:, cs]], axis=0)
        u_hi = ucat.astype(BF16)
        u_lo = (ucat - u_hi.astype(F32)).astype(BF16)
        lo = jnp.clip(pt - w // 2, 0, seq)
        hi = jnp.clip(pt + w // 2, 0, seq)
        band = jnp.where((ps >= lo) & (ps < hi), 1.0, 0.0).astype(BF16)
        cnt = (jnp.clip(pt1 + w // 2, 0, seq) - jnp.clip(pt1 - w // 2, 0, seq)).astype(F32)
        tot = (jnp.dot(band, u_hi, preferred_element_type=F32)
               + jnp.dot(band, u_lo, preferred_element_type=F32))
        dlt = tot / cnt - uc
        y = jnp.dot(dlt.astype(BF16), wp_ref[g], preferred_element_type=F32)
        o_ref[:, cs] = (y * ps_ref[:, cs]).astype(BF16)


def _pool(u, w_pool_b, pool_scale, *, seq):
    m, pw = u.shape
    pgw = pw // len(POOL_WINDOWS)
    bm = min(256, seq)
    hb = bm // POOL_HALO
    nh = m // POOL_HALO
    kern = functools.partial(_pool_kernel, seq=seq, pgw=pgw)
    return pl.pallas_call(
        kern,
        grid=(m // bm,),
        in_specs=[pl.BlockSpec((POOL_HALO, pw), lambda i: (jnp.maximum(i * hb - 1, 0), 0)),
                  pl.BlockSpec((bm, pw), lambda i: (i, 0)),
                  pl.BlockSpec((POOL_HALO, pw), lambda i: (jnp.minimum((i + 1) * hb, nh - 1), 0)),
                  pl.BlockSpec((len(POOL_WINDOWS), pgw, pgw), lambda i: (0, 0, 0)),
                  pl.BlockSpec((1, pw), lambda i: (0, 0))],
        out_specs=pl.BlockSpec((bm, pw), lambda i: (i, 0)),
        out_shape=jax.ShapeDtypeStruct((m, pw), BF16),
        compiler_params=_cparams(("parallel",)),
        name="pool",
    )(u, u, u, w_pool_b, pool_scale)


def _outproj_kernel(a_ref, p_ref, wa_ref, wp_ref, x_ref, g_ref, o_ref):
    mix = (jnp.dot(a_ref[...], wa_ref[...], preferred_element_type=F32)
           + jnp.dot(p_ref[...], wp_ref[...], preferred_element_type=F32))
    o_ref[...] = x_ref[...] + g_ref[...] * mix


def _outproj(attn, pool, w_out_b, x2, gate, *, seq):
    m, d = x2.shape
    aw = attn.shape[1]
    bm = min(1024, seq)
    bn = min(512, d)
    tpb = seq // bm
    return pl.pallas_call(
        _outproj_kernel,
        grid=(m // bm, d // bn),
        in_specs=[pl.BlockSpec((bm, aw), lambda i, j: (i, 0)),
                  pl.BlockSpec((bm, aw), lambda i, j: (i, 0)),
                  pl.BlockSpec((aw, bn), lambda i, j: (0, j)),
                  pl.BlockSpec((aw, bn), lambda i, j: (1, j)),
                  pl.BlockSpec((bm, bn), lambda i, j: (i, j)),
                  pl.BlockSpec((None, 1, bn), lambda i, j: (i // tpb, 0, j))],
        out_specs=pl.BlockSpec((bm, bn), lambda i, j: (i, j)),
        out_shape=jax.ShapeDtypeStruct((m, d), F32),
        compiler_params=_cparams(("parallel", "parallel")),
        name="outproj",
    )(attn, pool, w_out_b, w_out_b, x2, gate)


def _router_kernel(x_ref, sh_ref, sc_ref, g2_ref, wr_ref, br_ref, tri_ref,
                   hp_ref, mi_ref, mw_ref, cnt_ref, carry_ref, *, n_exp):
    i = pl.program_id(0)

    @pl.when(i == 0)
    def _():
        carry_ref[...] = jnp.zeros_like(carry_ref)

    h = _norm_mod(x_ref[...], g2_ref[...], sh_ref[...], sc_ref[...])
    bm, d = h.shape
    hp_ref[...] = _pack_pair(h[:, :d // 2], h[:, d // 2:])
    logits = lax.dot_general(wr_ref[...], h.astype(BF16), (((1,), (1,)), ((), ())),
                             preferred_element_type=F32) + br_ref[...]
    eidx = lax.broadcasted_iota(I32, (n_exp, bm), 0)
    l = logits
    tops, idxs, sels = [], [], []
    for _ in range(TOP_K):
        mx = jnp.max(l, axis=0, keepdims=True)
        ix = jnp.min(jnp.where(l == mx, eidx, n_exp), axis=0, keepdims=True)
        sel = eidx == ix
        tops.append(mx)
        idxs.append(ix)
        sels.append(sel)
        l = jnp.where(sel, -jnp.inf, l)
    es = [jnp.exp(t - tops[0]) for t in tops]
    den = es[0] + es[1] + es[2] + es[3]
    member = jnp.where(sels[0] | sels[1] | sels[2] | sels[3], 1.0, 0.0)
    before = jnp.dot(member.astype(BF16), tri_ref[...], preferred_element_type=F32) + carry_ref[:, 0:1]
    ranks = [jnp.sum(jnp.where(s, before, 0.0), axis=0, keepdims=True).astype(I32) for s in sels]
    carry_ref[...] = carry_ref[...] + jnp.sum(member, axis=1, keepdims=True)
    mi_ref[...] = jnp.concatenate(idxs + ranks, axis=0)
    mw_ref[...] = jnp.concatenate([e / den for e in es] + [jnp.zeros((TOP_K, bm), F32)], axis=0)
    cnt_ref[...] = carry_ref[...]


def _router(x1, shift, scale, g2, w_rt_b, b_r, tri, *, seq):
    m, d = x1.shape
    n_exp = w_rt_b.shape[0]
    bm = tri.shape[0]
    tpb = seq // bm
    kern = functools.partial(_router_kernel, n_exp=n_exp)
    return pl.pallas_call(
        kern,
        grid=(m // bm,),
        in_specs=[pl.BlockSpec((bm, d), lambda i: (i, 0)),
                  pl.BlockSpec((None, 1, d), lambda i: (i // tpb, 0, 0)),
                  pl.BlockSpec((None, 1, d), lambda i: (i // tpb, 0, 0)),
                  pl.BlockSpec((1, d), lambda i: (0, 0)),
                  pl.BlockSpec((n_exp, d), lambda i: (0, 0)),
                  pl.BlockSpec((n_exp, 1), lambda i: (0, 0)),
                  pl.BlockSpec((bm, bm), lambda i: (0, 0))],
        out_specs=[pl.BlockSpec((bm, d // 2), lambda i: (i, 0)),
                   pl.BlockSpec((2 * TOP_K, bm), lambda i: (0, i)),
                   pl.BlockSpec((2 * TOP_K, bm), lambda i: (0, i)),
                   pl.BlockSpec((n_exp, LANES), lambda i: (0, 0))],
        out_shape=[jax.ShapeDtypeStruct((m, d // 2), U32),
                   jax.ShapeDtypeStruct((2 * TOP_K, m), I32),
                   jax.ShapeDtypeStruct((2 * TOP_K, m), F32),
                   jax.ShapeDtypeStruct((n_exp, LANES), F32)],
        scratch_shapes=[pltpu.VMEM((n_exp, LANES), F32)],
        compiler_params=_cparams(("arbitrary",)),
        name="router",
    )(x1, shift, scale, g2, w_rt_b, b_r, tri)


def _row_copy(src, s_row, dst, d_row, sem):
    return pltpu.make_async_copy(src.at[pl.ds(s_row, 1)], dst.at[pl.ds(d_row, 1)], sem)


ISSUE_UNROLL = 8
COMBINE_ROWS = 32


def _dispatch_kernel(pos_ref, h_ref, xs_ref, pos_s, sem_p, sem):
    bd = h_ref.shape[0]
    cp = pltpu.make_async_copy(pos_ref, pos_s, sem_p)
    cp.start()
    cp.wait()

    def issue(tb, c):
        for k in range(ISSUE_UNROLL):
            t = tb * ISSUE_UNROLL + k
            for s in range(TOP_K):
                _row_copy(h_ref, t, xs_ref, pos_s[s, t], sem).start()
        return c

    lax.fori_loop(0, bd // ISSUE_UNROLL, issue, 0)
    for s in range(TOP_K):
        pltpu.make_async_copy(h_ref, xs_ref.at[pl.ds(0, bd)], sem).wait()


def _dispatch(pos8, hp, *, rows, bd):
    m, dh = hp.shape
    return pl.pallas_call(
        _dispatch_kernel,
        grid=(m // bd,),
        in_specs=[pl.BlockSpec((2 * TOP_K, bd), lambda i: (0, i)),
                  pl.BlockSpec((bd, dh), lambda i: (i, 0))],
        out_specs=pl.BlockSpec(memory_space=pl.ANY),
        out_shape=jax.ShapeDtypeStruct((rows, dh), U32),
        scratch_shapes=[pltpu.SMEM((2 * TOP_K, bd), I32), pltpu.SemaphoreType.DMA(()),
                        pltpu.SemaphoreType.DMA(())],
        compiler_params=_cparams(("arbitrary",)),
        name="dispatch",
    )(pos8, hp)


N_ITEM_ARRAYS = 9


def _group_weights(l, ie, ic, if_ref, ihn, ine, inc, hbm_refs, stg, dsts, sem, width):
    def copies(e, c):
        col = pl.multiple_of(c * width, width)
        return [pltpu.make_async_copy(h.at[e, :, pl.ds(col, width)], stg.at[k], sem.at[k])
                for k, h in enumerate(hbm_refs)]

    @pl.when(l == 0)
    def _():
        for cp in copies(ie[0], ic[0]):
            cp.start()

    @pl.when(if_ref[l] == 1)
    def _():
        for cp in copies(ie[l], ic[l]):
            cp.wait()
        for k, dst in enumerate(dsts):
            dst[...] = stg[k].astype(BF16)

        @pl.when(ihn[l] == 1)
        def _():
            for cp in copies(ine[l], inc[l]):
                cp.start()


def _moe_up_kernel(ie, ic, it, if_ref, iv_ref, in_ref, ihn, ine, inc,
                   xs_ref, wg_hbm, wu_hbm, bg_ref, bu_ref, a_ref, stg, wg_s, wu_s, sem):
    l = pl.program_id(0)
    _group_weights(l, ie, ic, if_ref, ihn, ine, inc, (wg_hbm, wu_hbm), stg, (wg_s, wu_s), sem, wg_s.shape[1])

    @pl.when(iv_ref[l] == 1)
    def _():
        w = xs_ref[...]
        row = lax.broadcasted_iota(I32, w.shape, 0)
        w = jnp.where(row < in_ref[l], w, jnp.uint32(0))
        lo, hi = _unpack_pair(w)
        x = jnp.concatenate([lo.astype(BF16), hi.astype(BF16)], axis=1)
        gate = jnp.minimum(jnp.dot(x, wg_s[...], preferred_element_type=F32) + bg_ref[...], SWIGLU_LIMIT)
        lin = jnp.clip(jnp.dot(x, wu_s[...], preferred_element_type=F32) + bu_ref[...],
                       -SWIGLU_LIMIT, SWIGLU_LIMIT)
        a_ref[...] = (gate * jax.nn.sigmoid(SWIGLU_ALPHA * gate) * (lin + 1.0)).astype(BF16)


def _moe_up(items, xs, w_gate, w_up, b_gate, b_up, *, bmoe, nck):
    rows, dh = xs.shape
    n_exp, d, ff = w_gate.shape
    fc = ff // nck
    n_items = items[0].shape[0]
    grid_spec = pltpu.PrefetchScalarGridSpec(
        num_scalar_prefetch=N_ITEM_ARRAYS,
        grid=(n_items,),
        in_specs=[pl.BlockSpec((bmoe, dh), lambda l, ie, ic, it, *_: (it[l], 0)),
                  pl.BlockSpec(memory_space=pl.ANY),
                  pl.BlockSpec(memory_space=pl.ANY),
                  pl.BlockSpec((None, 1, fc), lambda l, ie, ic, it, *_: (ie[l], 0, ic[l])),
                  pl.BlockSpec((None, 1, fc), lambda l, ie, ic, it, *_: (ie[l], 0, ic[l]))],
        out_specs=pl.BlockSpec((bmoe, fc), lambda l, ie, ic, it, *_: (it[l], ic[l])),
        scratch_shapes=[pltpu.VMEM((2, d, fc), F32), pltpu.VMEM((d, fc), BF16), pltpu.VMEM((d, fc), BF16),
                        pltpu.SemaphoreType.DMA((2,))],
    )
    return pl.pallas_call(
        _moe_up_kernel,
        grid_spec=grid_spec,
        out_shape=jax.ShapeDtypeStruct((rows, ff), BF16),
        compiler_params=_cparams(("arbitrary",)),
        name="moe_up",
    )(*items, xs, w_gate, w_up, b_gate, b_up)


def _moe_down_kernel(ie, ic, it, if_ref, iv_ref, in_ref, ihn, ine, inc,
                     a_ref, wd_hbm, bd_ref, y_ref, stg, wd_s, sem):
    l = pl.program_id(0)
    _group_weights(l, ie, ic, if_ref, ihn, ine, inc, (wd_hbm,), stg, (wd_s,), sem, wd_s.shape[1])

    @pl.when(iv_ref[l] == 1)
    def _():
        y = jnp.dot(a_ref[...], wd_s[...], preferred_element_type=F32) + bd_ref[...]
        half = y.shape[1] // 2
        y_ref[...] = _pack_pair(y[:, :half], y[:, half:])


def _moe_down(items, act, w_down, b_down, *, bmoe, nck):
    rows, ff = act.shape
    n_exp, _, d = w_down.shape
    dc = d // nck
    n_items = items[0].shape[0]
    grid_spec = pltpu.PrefetchScalarGridSpec(
        num_scalar_prefetch=N_ITEM_ARRAYS,
        grid=(n_items,),
        in_specs=[pl.BlockSpec((bmoe, ff), lambda l, ie, ic, it, *_: (it[l], 0)),
                  pl.BlockSpec(memory_space=pl.ANY),
                  pl.BlockSpec((None, 1, dc), lambda l, ie, ic, it, *_: (ie[l], 0, ic[l]))],
        out_specs=pl.BlockSpec((bmoe, dc // 2), lambda l, ie, ic, it, *_: (it[l], ic[l])),
        scratch_shapes=[pltpu.VMEM((1, ff, dc), F32), pltpu.VMEM((ff, dc), BF16), pltpu.SemaphoreType.DMA((1,))],
    )
    return pl.pallas_call(
        _moe_down_kernel,
        grid_spec=grid_spec,
        out_shape=jax.ShapeDtypeStruct((rows, d // 2), U32),
        compiler_params=_cparams(("arbitrary",)),
        name="moe_down",
    )(*items, act, w_down, b_down)


def _combine_kernel(pos_ref, posn_ref, w_ref, x_ref, g_ref, ys_ref, o_ref, pos_s, buf, sem_p, sem, *, nck):
    i = pl.program_id(0)
    bc = x_ref.shape[0]
    slot = i % 2

    def gather(p_ref, sl):
        cp = pltpu.make_async_copy(p_ref, pos_s, sem_p)
        cp.start()
        cp.wait()

        def issue(tb, c):
            for k in range(ISSUE_UNROLL):
                t = tb * ISSUE_UNROLL + k
                for s in range(TOP_K):
                    _row_copy(ys_ref, pos_s[s, t], buf.at[sl, s], t, sem.at[sl]).start()
            return c

        lax.fori_loop(0, bc // ISSUE_UNROLL, issue, 0)

    @pl.when(i == 0)
    def _():
        gather(pos_ref, 0)

    @pl.when(i + 1 < pl.num_programs(0))
    def _():
        gather(posn_ref, 1 - slot)

    for s in range(TOP_K):
        pltpu.make_async_copy(ys_ref.at[pl.ds(0, bc)], buf.at[slot, s], sem.at[slot]).wait()

    dq = x_ref.shape[1] // (2 * nck)
    rc = COMBINE_ROWS

    def rows_body(rb, carry):
        r0 = pl.multiple_of(rb * rc, rc)
        wrow = w_ref[pl.ds(r0, rc), :]
        ws = [jnp.broadcast_to(wrow[:, s:s + 1], (rc, LANES)) for s in range(TOP_K)]
        for k0 in range(0, nck * dq, LANES):
            lo_acc = hi_acc = None
            for s in range(TOP_K):
                lo, hi = _unpack_pair(buf[slot, s, pl.ds(r0, rc), k0:k0 + LANES])
                lo_acc = ws[s] * lo if lo_acc is None else lo_acc + ws[s] * lo
                hi_acc = ws[s] * hi if hi_acc is None else hi_acc + ws[s] * hi
            c = k0 // dq
            col = c * 2 * dq + (k0 - c * dq)
            for cc, acc in ((col, lo_acc), (col + dq, hi_acc)):
                cs = slice(cc, cc + LANES)
                o_ref[pl.ds(r0, rc), cs] = x_ref[pl.ds(r0, rc), cs] + g_ref[:, cs] * acc
        return carry

    lax.fori_loop(0, bc // rc, rows_body, 0)


def _combine(pos8, wtok, x1, gate, ys, *, seq, nck, bc):
    m, d = x1.shape
    tpb = seq // bc
    nsteps = m // bc
    kern = functools.partial(_combine_kernel, nck=nck)
    return pl.pallas_call(
        kern,
        grid=(nsteps,),
        in_specs=[pl.BlockSpec((2 * TOP_K, bc), lambda i: (0, i)),
                  pl.BlockSpec((2 * TOP_K, bc), lambda i: (0, jnp.minimum(i + 1, nsteps - 1))),
                  pl.BlockSpec((bc, 2 * TOP_K), lambda i: (i, 0)),
                  pl.BlockSpec((bc, d), lambda i: (i, 0)),
                  pl.BlockSpec((None, 1, d), lambda i: (i // tpb, 0, 0)),
                  pl.BlockSpec(memory_space=pl.ANY)],
        out_specs=pl.BlockSpec((bc, d), lambda i: (i, 0)),
        out_shape=jax.ShapeDtypeStruct((m, d), F32),
        scratch_shapes=[pltpu.SMEM((2 * TOP_K, bc), I32), pltpu.VMEM((2, TOP_K, bc, d // 2), U32),
                        pltpu.SemaphoreType.DMA(()), pltpu.SemaphoreType.DMA((2,))],
        compiler_params=_cparams(("arbitrary",)),
        name="combine",
    )(pos8, pos8, wtok, x1, gate, ys)


def _rope_tables(seq):
    t = jnp.arange(seq)
    row = (t // GRID_W).astype(F32)
    col = (t % GRID_W).astype(F32)
    inv_freq = ROPE_BASE ** (-jnp.arange(0, ROPE_HALF, 2, dtype=F32) / ROPE_HALF)
    ar = row[:, None] * inv_freq[None, :]
    ac = col[:, None] * inv_freq[None, :]
    cos = jnp.concatenate([jnp.cos(ar), jnp.cos(ar), jnp.cos(ac), jnp.cos(ac)], axis=1)
    sin = jnp.concatenate([-jnp.sin(ar), jnp.sin(ar), -jnp.sin(ac), jnp.sin(ac)], axis=1)
    rep = LANES // HEAD_DIM
    return jnp.tile(cos, (1, rep)), jnp.tile(sin, (1, rep))


def _window_bias(seq):
    i = jnp.arange(BLOCK)[:, None]
    j = jnp.arange(3 * BLOCK)[None, :]
    band = (j >= i) & (j <= i + 2 * WINDOW)
    first = band & (j >= BLOCK)
    last = band & (j < 2 * BLOCK)
    both = first & last
    variants = jnp.stack([both if seq == BLOCK else first, band, last])
    return jnp.where(variants, 0.0, NEG_INF).astype(F32)


def _work_items(counts, *, bmoe, nck, n_tiles):
    n_exp = counts.shape[0]
    tiles_e = (counts + bmoe - 1) // bmoe
    tile_end = jnp.cumsum(tiles_e)
    tile_start = tile_end - tiles_e
    item_end = nck * tile_end
    total = item_end[-1]
    l = jnp.arange(nck * n_tiles, dtype=I32)
    valid = l < total
    lc = jnp.minimum(l, total - 1)
    e = jnp.minimum(jnp.sum((item_end[None, :] <= lc[:, None]).astype(I32), axis=1), n_exp - 1)
    is_e = e[:, None] == jnp.arange(n_exp, dtype=I32)[None, :]
    of_e = lambda v: jnp.sum(jnp.where(is_e, v[None, :], 0), axis=1)
    nt = jnp.maximum(of_e(tiles_e), 1)
    ts = of_e(tile_start)
    within = lc - nck * ts
    c = within // nt
    k = within % nt
    t = ts + k
    first = (k == 0) & valid
    nrows = jnp.clip(of_e(counts) - k * bmoe, 0, bmoe)
    e_after = jnp.sum((item_end[None, :] <= of_e(item_end)[:, None]).astype(I32), axis=1)
    same_e = c + 1 < nck
    has_next = valid & (same_e | (e_after < n_exp))
    next_e = jnp.where(same_e, e, jnp.minimum(e_after, n_exp - 1))
    next_c = jnp.where(same_e, c + 1, 0)
    as_i = lambda a: a.astype(I32)
    return (as_i(e), as_i(c), as_i(t), as_i(first), as_i(valid), as_i(nrows),
            as_i(has_next), as_i(next_e), as_i(next_c)), tile_start * bmoe


def kernel(x, c, ctx, c_ctx, w_ada, b_ada, norm1_g, w_in, q_norm_g, k_norm_g, sinks, w_pool, pool_scale,
           w_out, norm2_g, w_router, b_router, w_gate, b_gate, w_up, b_up, w_down, b_down):
    assert w_ada.shape[0] == 1, "one layer: the context stream is only read, never updated"
    batch, seq, d = x.shape
    ctx_len = ctx.shape[1]
    aw = d // 2
    nq = aw // HEAD_DIM
    nkv = max(1, nq // 8)
    grp = nq // nkv
    kvw = nkv * HEAD_DIM
    pw = d - aw
    n_exp, _, ff = w_gate.shape[1:]
    m = batch * seq
    assert seq % BLOCK == 0 and seq % GRID_W == 0 and kvw % LANES == 0 and grp % 2 == 0 and aw == pw

    c8 = jnp.zeros((8, d), F32).at[:batch].set(c).at[batch].set(c_ctx)
    mod = _ada(c8, w_ada[0], b_ada)
    mod_b = mod[:batch].reshape(batch, N_MOD, 1, d)
    mod_c = mod[batch].reshape(N_MOD, 1, d)

    w_in_b = w_in[0].astype(BF16)
    w_out_b = w_out[0].astype(BF16)
    cos_t, sin_t = _rope_tables(seq)
    qg = jnp.tile(q_norm_g[0].astype(F32), LANES // HEAD_DIM)[None, :]
    kg = jnp.tile(k_norm_g[0].astype(F32), LANES // HEAD_DIM)[None, :]
    swap16 = lambda g: g.reshape(LANES // ROPE_HALF, 2, ROPE_HALF // 2)[:, ::-1, :].reshape(1, LANES)
    qs = HEAD_DIM ** -0.5
    qa, qb = cos_t * (qg * qs), sin_t * (swap16(qg) * qs)
    ka, kb = cos_t * kg, sin_t * swap16(kg)
    bn = 2 * kvw
    lane_head = jnp.arange(bn) // HEAD_DIM
    bd = jnp.where(lane_head[:, None] == lane_head[None, :], 1.0 / HEAD_DIM, 0.0).astype(BF16)
    g1 = norm1_g[0][None, :]

    x2 = x.reshape(m, d)
    q, kv, u = _inproj(x2, mod_b[:, 0], mod_b[:, 1], g1, w_in_b, qa, qb, ka, kb, bd,
                       seq=seq, aw=aw, kvw=kvw, pw=pw)
    kvc = _ctx_kv(ctx.reshape(batch * ctx_len, d), mod_c[0], mod_c[1], g1,
                  w_in_b[:, aw:aw + bn], kg, bd, kvw=kvw)
    attn = _attn(q, kv, kvc, _window_bias(seq), sinks[0].astype(F32),
                 batch=batch, seq=seq, ctx_len=ctx_len, nkv=nkv, grp=grp, kvw=kvw)
    pool = _pool(u, w_pool[0].astype(BF16), pool_scale, seq=seq)
    x1 = _outproj(attn, pool, w_out_b, x2, mod_b[:, 2], seq=seq)

    bm5 = min(512, seq)
    tri = (jnp.arange(bm5)[:, None] < jnp.arange(bm5)[None, :]).astype(BF16)
    hp, meta_i, meta_w, cnt = _router(x1, mod_b[:, 3], mod_b[:, 4], norm2_g[0][None, :],
                                      w_router[0].T.astype(BF16), b_router[0][:, None], tri, seq=seq)
    counts = cnt[:, 0].astype(I32)
    bmoe = 256 if m * TOP_K >= 512 * n_exp else 128
    nck_up, nck_dn = 2, 1
    n_tiles = (m * TOP_K) // bmoe + n_exp
    items_up, row_start = _work_items(counts, bmoe=bmoe, nck=nck_up, n_tiles=n_tiles)
    items_dn, _ = _work_items(counts, bmoe=bmoe, nck=nck_dn, n_tiles=n_tiles)
    idx, rank = meta_i[:TOP_K], meta_i[TOP_K:]
    start_of = jnp.sum(jnp.where(idx[..., None] == jnp.arange(n_exp), row_start, 0), axis=-1)
    pos8 = jnp.concatenate([start_of + rank, jnp.zeros_like(rank)], axis=0).astype(I32)

    xs = _dispatch(pos8, hp, rows=n_tiles * bmoe, bd=min(256, seq))
    act = _moe_up(items_up, xs, w_gate[0], w_up[0], b_gate[0][:, None, :], b_up[0][:, None, :],
                  bmoe=bmoe, nck=nck_up)
    ys = _moe_down(items_dn, act, w_down[0], b_down[0][:, None, :], bmoe=bmoe, nck=nck_dn)
    out = _combine(pos8, meta_w.T, x1, mod_b[:, 5], ys, seq=seq, nck=nck_dn, bc=min(256, seq))
    return out.reshape(batch, seq, d)
```

```python
import functools

import jax
import jax.numpy as jnp
from jax import lax
from jax.experimental import pallas as pl
from jax.experimental.pallas import tpu as pltpu

F32 = jnp.float32
BF16 = jnp.bfloat16
U32 = jnp.uint32
I32 = jnp.int32

HEAD_DIM = 64
LANES = 128
GRID_W = 64
WINDOW = 128
BLOCK = 128
ROPE_HALF = HEAD_DIM // 2
ROPE_BASE = 10000.0
POOL_WINDOWS = (2, 4, 8, 16)
POOL_HALO = 8
TOP_K = 4
N_MOD = 6
NORM_EPS = 1e-6
NEG_INF = -1e30
SWIGLU_LIMIT = 7.0
SWIGLU_ALPHA = 1.702
VMEM_LIMIT = 56 * 1024 * 1024


def _cparams(sem):
    return pltpu.CompilerParams(dimension_semantics=sem, vmem_limit_bytes=VMEM_LIMIT)


def _pack_pair(lo, hi):
    a = pltpu.bitcast(lo.astype(BF16).astype(F32), U32) >> 16
    b = pltpu.bitcast(hi.astype(BF16).astype(F32), U32) & jnp.uint32(0xFFFF0000)
    return a | b


def _unpack_pair(w):
    lo = pltpu.bitcast(w << 16, F32)
    hi = pltpu.bitcast(w & jnp.uint32(0xFFFF0000), F32)
    return lo, hi


def _ada_kernel(c_ref, w_ref, b_ref, o_ref):
    c = c_ref[...]
    s = c * jax.nn.sigmoid(c)
    hi = s.astype(BF16)
    lo = (s - hi.astype(F32)).astype(BF16)
    lhs = jnp.concatenate([hi, lo], axis=0)
    r = jnp.dot(lhs, w_ref[...].astype(BF16), preferred_element_type=F32)
    o_ref[...] = r[:8] + r[8:] + b_ref[...]


def _ada(c8, w_ada, b_ada):
    d, n = w_ada.shape
    bn = min(512, n)
    return pl.pallas_call(
        _ada_kernel,
        grid=(n // bn,),
        in_specs=[pl.BlockSpec((8, d), lambda j: (0, 0)),
                  pl.BlockSpec((d, bn), lambda j: (0, j)),
                  pl.BlockSpec((1, bn), lambda j: (0, j))],
        out_specs=pl.BlockSpec((8, bn), lambda j: (0, j)),
        out_shape=jax.ShapeDtypeStruct((8, n), F32),
        compiler_params=_cparams(("parallel",)),
        name="ada",
    )(c8, w_ada, b_ada)


def _norm_mod(x, g, shift, scale):
    ms = jnp.mean(x * x, axis=-1, keepdims=True)
    y = x * lax.rsqrt(ms + NORM_EPS) * g
    return y * (1.0 + scale) + shift


def _head_rnorm(z, bd):
    msq = jnp.dot((z * z).astype(BF16), bd, preferred_element_type=F32)
    return lax.rsqrt(msq + NORM_EPS)


def _rope(z, ta, tb):
    rows = z.shape[0]
    lane = lax.broadcasted_iota(I32, (rows, LANES), 1)
    first = (lane % ROPE_HALF) < (ROPE_HALF // 2)
    outs = []
    for s in range(z.shape[1] // LANES):
        zs = z[:, s * LANES:(s + 1) * LANES]
        up = pltpu.roll(zs, LANES - ROPE_HALF // 2, 1)
        dn = pltpu.roll(zs, ROPE_HALF // 2, 1)
        outs.append(zs * ta + jnp.where(first, up, dn) * tb)
    return outs[0] if len(outs) == 1 else jnp.concatenate(outs, axis=1)


def _tile_gain(g_ref, width):
    g = g_ref[...]
    return g if width == LANES else jnp.concatenate([g] * (width // LANES), axis=1)


def _inproj_kernel(x_ref, sh_ref, sc_ref, g1_ref, w_ref, qa_ref, qb_ref, ka_ref, kb_ref, bd_ref,
                   q_ref, kv_ref, u_ref, h_ref, zs_ref, *, nq_t, kvw):
    j = pl.program_id(1)
    cur = j % 2

    def mm():
        return jnp.dot(h_ref[...], w_ref[...], preferred_element_type=F32)

    @pl.when(j == 0)
    def _():
        h_ref[...] = _norm_mod(x_ref[...], g1_ref[...], sh_ref[...], sc_ref[...]).astype(BF16)
        zs_ref[0] = mm()

    @pl.when((j >= 1) & (j <= nq_t))
    def _():
        zp = zs_ref[1 - cur]
        zs_ref[cur] = mm()
        q_ref[...] = (_rope(zp, qa_ref[...], qb_ref[...]) * _head_rnorm(zp, bd_ref[...])).astype(BF16)

    @pl.when(j == nq_t + 1)
    def _():
        zp = zs_ref[1 - cur]
        u_ref[...] = mm()
        zk = zp[:, :kvw]
        kv_ref[:, :kvw] = (_rope(zk, ka_ref[...], kb_ref[...]) * _head_rnorm(zk, bd_ref[:kvw, :kvw])).astype(BF16)
        kv_ref[:, kvw:] = zp[:, kvw:].astype(BF16)

    @pl.when(j > nq_t + 1)
    def _():
        u_ref[...] = mm()


def _inproj(x2, shift, scale, g1, w_in_b, qa, qb, ka, kb, bd, *, seq, aw, kvw, pw):
    m, d = x2.shape
    bn = 2 * kvw
    nq_t, nu_t = aw // bn, pw // bn
    bm = min(512, seq)
    tpb = seq // bm
    kern = functools.partial(_inproj_kernel, nq_t=nq_t, kvw=kvw)
    return pl.pallas_call(
        kern,
        grid=(m // bm, nq_t + 1 + nu_t),
        in_specs=[pl.BlockSpec((bm, d), lambda i, j: (i, 0)),
                  pl.BlockSpec((None, 1, d), lambda i, j: (i // tpb, 0, 0)),
                  pl.BlockSpec((None, 1, d), lambda i, j: (i // tpb, 0, 0)),
                  pl.BlockSpec((1, d), lambda i, j: (0, 0)),
                  pl.BlockSpec((d, bn), lambda i, j: (0, j)),
                  pl.BlockSpec((bm, LANES), lambda i, j: (i % tpb, 0)),
                  pl.BlockSpec((bm, LANES), lambda i, j: (i % tpb, 0)),
                  pl.BlockSpec((bm, LANES), lambda i, j: (i % tpb, 0)),
                  pl.BlockSpec((bm, LANES), lambda i, j: (i % tpb, 0)),
                  pl.BlockSpec((bn, bn), lambda i, j: (0, 0))],
        out_specs=[pl.BlockSpec((bm, bn), lambda i, j: (i, jnp.clip(j - 1, 0, nq_t - 1))),
                   pl.BlockSpec((bm, bn), lambda i, j: (i, 0)),
                   pl.BlockSpec((bm, bn), lambda i, j: (i, jnp.clip(j - nq_t - 1, 0, nu_t - 1)))],
        out_shape=[jax.ShapeDtypeStruct((m, aw), BF16),
                   jax.ShapeDtypeStruct((m, 2 * kvw), BF16),
                   jax.ShapeDtypeStruct((m, pw), F32)],
        scratch_shapes=[pltpu.VMEM((bm, d), BF16), pltpu.VMEM((2, bm, bn), F32)],
        compiler_params=_cparams(("parallel", "arbitrary")),
        name="inproj",
    )(x2, shift, scale, g1, w_in_b, qa, qb, ka, kb, bd)


def _ctx_kv_kernel(x_ref, sh_ref, sc_ref, g1_ref, w_ref, kg_ref, bd_ref, kv_ref, *, kvw):
    h = _norm_mod(x_ref[...], g1_ref[...], sh_ref[...], sc_ref[...]).astype(BF16)
    z = jnp.dot(h, w_ref[...], preferred_element_type=F32)
    zk = z[:, :kvw]
    kv_ref[:, :kvw] = (zk * _head_rnorm(zk, bd_ref[:kvw, :kvw]) * _tile_gain(kg_ref, kvw)).astype(BF16)
    kv_ref[:, kvw:] = z[:, kvw:].astype(BF16)


def _ctx_kv(c2, shift, scale, g1, w_kv_b, kg, bd, *, kvw):
    m, d = c2.shape
    bn = 2 * kvw
    bm = min(256, m)
    kern = functools.partial(_ctx_kv_kernel, kvw=kvw)
    return pl.pallas_call(
        kern,
        grid=(m // bm,),
        in_specs=[pl.BlockSpec((bm, d), lambda i: (i, 0)),
                  pl.BlockSpec((1, d), lambda i: (0, 0)),
                  pl.BlockSpec((1, d), lambda i: (0, 0)),
                  pl.BlockSpec((1, d), lambda i: (0, 0)),
                  pl.BlockSpec((d, bn), lambda i: (0, 0)),
                  pl.BlockSpec((1, LANES), lambda i: (0, 0)),
                  pl.BlockSpec((bn, bn), lambda i: (0, 0))],
        out_specs=pl.BlockSpec((bm, bn), lambda i: (i, 0)),
        out_shape=jax.ShapeDtypeStruct((m, bn), BF16),
        compiler_params=_cparams(("parallel",)),
        name="ctx_kv",
    )(c2, shift, scale, g1, w_kv_b, kg, bd)


def _attn_kernel(sink_ref, q_ref, kp_ref, kc_ref, kn_ref, kx_ref, bias_ref, o_ref, *, nkv, grp, kvw, ctx_len):
    npair = grp // 2
    nk = ctx_len + 3 * BLOCK
    lane_kv = lax.broadcasted_iota(I32, (nk, LANES), 1)
    lane_q = lax.broadcasted_iota(I32, (BLOCK, LANES), 1)
    bias = bias_ref[...]
    for h in range(nkv):
        slab = (h // 2) * LANES
        mine = (lane_kv // HEAD_DIM) == (h % 2)

        def window(off):
            w = jnp.concatenate([kx_ref[:, off + slab:off + slab + LANES],
                                 kp_ref[:, off + slab:off + slab + LANES],
                                 kc_ref[:, off + slab:off + slab + LANES],
                                 kn_ref[:, off + slab:off + slab + LANES]], axis=0).astype(F32)
            wz = jnp.where(mine, w, 0.0)
            wr = pltpu.roll(wz, HEAD_DIM, 1)
            lo, hi = (wz, wr) if h % 2 == 0 else (wr, wz)
            return lo.astype(BF16), hi.astype(BF16)

        k_e, k_o = window(0)
        v_e, v_o = window(kvw)
        q0 = h * grp * HEAD_DIM
        q4 = jnp.concatenate([q_ref[:, q0 + p * LANES:q0 + (p + 1) * LANES] for p in range(npair)], axis=0)
        kcat = jnp.concatenate([k_e, k_o], axis=0)
        s_all = lax.dot_general(q4, kcat, (((1,), (1,)), ((), ())), preferred_element_type=F32)
        p_e, p_o, rinv = [], [], []
        for p in range(npair):
            halves = []
            for half in range(2):
                sink = sink_ref[h * grp + 2 * p + half]
                s = s_all[p * BLOCK:(p + 1) * BLOCK, half * nk:(half + 1) * nk]
                s = jnp.concatenate([s[:, :ctx_len], s[:, ctx_len:] + bias], axis=1)
                mx = jnp.maximum(jnp.max(s, axis=-1, keepdims=True), sink)
                e = jnp.exp(s - mx)
                den = jnp.sum(e, axis=-1, keepdims=True) + jnp.exp(sink - mx)
                halves.append((e.astype(BF16), 1.0 / den))
            p_e.append(halves[0][0])
            p_o.append(halves[1][0])
            rinv.append(jnp.where(lane_q < HEAD_DIM, halves[0][1], halves[1][1]))
        o = (jnp.dot(jnp.concatenate(p_e, axis=0), v_e, preferred_element_type=F32)
             + jnp.dot(jnp.concatenate(p_o, axis=0), v_o, preferred_element_type=F32))
        for p in range(npair):
            o_ref[:, q0 + p * LANES:q0 + (p + 1) * LANES] = (o[p * BLOCK:(p + 1) * BLOCK] * rinv[p]).astype(BF16)


def _attn(q, kv, kvc, bias, sinks, *, batch, seq, ctx_len, nkv, grp, kvw):
    m, aw = q.shape
    nb = seq // BLOCK
    kern = functools.partial(_attn_kernel, nkv=nkv, grp=grp, kvw=kvw, ctx_len=ctx_len)

    def sel(n):
        first = 1 - jnp.minimum(n, 1)
        last = 1 - jnp.minimum(nb - 1 - n, 1)
        return 1 - first + last

    return pl.pallas_call(
        kern,
        grid=(batch, nb),
        in_specs=[pl.BlockSpec(memory_space=pltpu.SMEM),
                  pl.BlockSpec((BLOCK, aw), lambda b, n: (b * nb + n, 0)),
                  pl.BlockSpec((BLOCK, 2 * kvw), lambda b, n: (b * nb + jnp.maximum(n - 1, 0), 0)),
                  pl.BlockSpec((BLOCK, 2 * kvw), lambda b, n: (b * nb + n, 0)),
                  pl.BlockSpec((BLOCK, 2 * kvw), lambda b, n: (b * nb + jnp.minimum(n + 1, nb - 1), 0)),
                  pl.BlockSpec((ctx_len, 2 * kvw), lambda b, n: (b, 0)),
                  pl.BlockSpec((None, BLOCK, 3 * BLOCK), lambda b, n: (sel(n), 0, 0))],
        out_specs=pl.BlockSpec((BLOCK, aw), lambda b, n: (b * nb + n, 0)),
        out_shape=jax.ShapeDtypeStruct((m, aw), BF16),
        compiler_params=_cparams(("parallel", "parallel")),
        name="attn",
    )(sinks, q, kv, kv, kv, kvc, bias)


def _pool_kernel(up_ref, uc_ref, un_ref, wp_ref, ps_ref, o_ref, *, seq, pgw):
    bm = uc_ref.shape[0]
    t0 = (pl.program_id(0) * bm) % seq
    rows = bm + 2 * POOL_HALO
    pt = t0 + lax.broadcasted_iota(I32, (bm, rows), 0)
    ps = t0 - POOL_HALO + lax.broadcasted_iota(I32, (bm, rows), 1)
    pt1 = t0 + lax.broadcasted_iota(I32, (bm, 1), 0)
    for g, w in enumerate(POOL_WINDOWS):
        cs = slice(g * pgw, (g + 1) * pgw)
        uc = uc_ref[:, cs]
        ucat = jnp.concatenate([up_ref[:, cs], uc, un_ref[:, cs]], axis=0)
        lo = jnp.clip(pt - w // 2, 0, seq)
        hi = jnp.clip(pt + w // 2, 0, seq)
        band = jnp.where((ps >= lo) & (ps < hi), 1.0, 0.0).astype(BF16)
        cnt = (jnp.clip(pt1 + w // 2, 0, seq) - jnp.clip(pt1 - w // 2, 0, seq)).astype(F32)
        tot = jnp.dot(band, ucat.astype(BF16), preferred_element_type=F32)
        dlt = tot / cnt - uc
        y = jnp.dot(dlt.astype(BF16), wp_ref[g], preferred_element_type=F32)
        o_ref[:, cs] = (y * ps_ref[:, cs]).astype(BF16)


def _pool(u, w_pool_b, pool_scale, *, seq):
    m, pw = u.shape
    pgw = pw // len(POOL_WINDOWS)
    bm = min(256, seq)
    hb = bm // POOL_HALO
    nh = m // POOL_HALO
    kern = functools.partial(_pool_kernel, seq=seq, pgw=pgw)
    return pl.pallas_call(
        kern,
        grid=(m // bm,),
        in_specs=[pl.BlockSpec((POOL_HALO, pw), lambda i: (jnp.maximum(i * hb - 1, 0), 0)),
                  pl.BlockSpec((bm, pw), lambda i: (i, 0)),
                  pl.BlockSpec((POOL_HALO, pw), lambda i: (jnp.minimum((i + 1) * hb, nh - 1), 0)),
                  pl.BlockSpec((len(POOL_WINDOWS), pgw, pgw), lambda i: (0, 0, 0)),
                  pl.BlockSpec((1, pw), lambda i: (0, 0))],
        out_specs=pl.BlockSpec((bm, pw), lambda i: (i, 0)),
        out_shape=jax.ShapeDtypeStruct((m, pw), BF16),
        compiler_params=_cparams(("parallel",)),
        name="pool",
    )(u, u, u, w_pool_b, pool_scale)


def _outproj_kernel(a_ref, p_ref, wa_ref, wp_ref, x_ref, g_ref, o_ref):
    mix = (jnp.dot(a_ref[...], wa_ref[...], preferred_element_type=F32)
           + jnp.dot(p_ref[...], wp_ref[...], preferred_element_type=F32))
    o_ref[...] = x_ref[...] + g_ref[...] * mix


def _outproj(attn, pool, w_out_b, x2, gate, *, seq):
    m, d = x2.shape
    aw = attn.shape[1]
    bm = min(1024, seq)
    bn = min(1024, d)
    tpb = seq // bm
    return pl.pallas_call(
        _outproj_kernel,
        grid=(m // bm, d // bn),
        in_specs=[pl.BlockSpec((bm, aw), lambda i, j: (i, 0)),
                  pl.BlockSpec((bm, aw), lambda i, j: (i, 0)),
                  pl.BlockSpec((aw, bn), lambda i, j: (0, j)),
                  pl.BlockSpec((aw, bn), lambda i, j: (1, j)),
                  pl.BlockSpec((bm, bn), lambda i, j: (i, j)),
                  pl.BlockSpec((None, 1, bn), lambda i, j: (i // tpb, 0, j))],
        out_specs=pl.BlockSpec((bm, bn), lambda i, j: (i, j)),
        out_shape=jax.ShapeDtypeStruct((m, d), F32),
        compiler_params=_cparams(("parallel", "parallel")),
        name="outproj",
    )(attn, pool, w_out_b, w_out_b, x2, gate)


def _router_kernel(x_ref, sh_ref, sc_ref, g2_ref, wr_ref, br_ref, tri_ref,
                   hp_ref, mi_ref, mw_ref, cnt_ref, carry_ref, *, n_exp):
    i = pl.program_id(0)

    @pl.when(i == 0)
    def _():
        carry_ref[...] = jnp.zeros_like(carry_ref)

    h = _norm_mod(x_ref[...], g2_ref[...], sh_ref[...], sc_ref[...])
    bm, d = h.shape
    hp_ref[...] = _pack_pair(h[:, :d // 2], h[:, d // 2:])
    logits = lax.dot_general(wr_ref[...], h.astype(BF16), (((1,), (1,)), ((), ())),
                             preferred_element_type=F32) + br_ref[...]
    eidx = lax.broadcasted_iota(I32, (n_exp, bm), 0)
    l = logits
    tops, idxs, sels = [], [], []
    for _ in range(TOP_K):
        mx = jnp.max(l, axis=0, keepdims=True)
        ix = jnp.min(jnp.where(l == mx, eidx, n_exp), axis=0, keepdims=True)
        sel = eidx == ix
        tops.append(mx)
        idxs.append(ix)
        sels.append(sel)
        l = jnp.where(sel, -jnp.inf, l)
    es = [jnp.exp(t - tops[0]) for t in tops]
    den = es[0] + es[1] + es[2] + es[3]
    member = jnp.where(sels[0] | sels[1] | sels[2] | sels[3], 1.0, 0.0)
    before = jnp.dot(member.astype(BF16), tri_ref[...], preferred_element_type=F32) + carry_ref[:, 0:1]
    ranks = [jnp.sum(jnp.where(s, before, 0.0), axis=0, keepdims=True).astype(I32) for s in sels]
    carry_ref[...] = carry_ref[...] + jnp.sum(member, axis=1, keepdims=True)
    mi_ref[...] = jnp.concatenate(idxs + ranks, axis=0)
    mw_ref[...] = jnp.concatenate([e / den for e in es] + [jnp.zeros((TOP_K, bm), F32)], axis=0)
    cnt_ref[...] = carry_ref[...]


def _router(x1, shift, scale, g2, w_rt_b, b_r, tri, *, seq):
    m, d = x1.shape
    n_exp = w_rt_b.shape[0]
    bm = tri.shape[0]
    tpb = seq // bm
    kern = functools.partial(_router_kernel, n_exp=n_exp)
    return pl.pallas_call(
        kern,
        grid=(m // bm,),
        in_specs=[pl.BlockSpec((bm, d), lambda i: (i, 0)),
                  pl.BlockSpec((None, 1, d), lambda i: (i // tpb, 0, 0)),
                  pl.BlockSpec((None, 1, d), lambda i: (i // tpb, 0, 0)),
                  pl.BlockSpec((1, d), lambda i: (0, 0)),
                  pl.BlockSpec((n_exp, d), lambda i: (0, 0)),
                  pl.BlockSpec((n_exp, 1), lambda i: (0, 0)),
                  pl.BlockSpec((bm, bm), lambda i: (0, 0))],
        out_specs=[pl.BlockSpec((bm, d // 2), lambda i: (i, 0)),
                   pl.BlockSpec((2 * TOP_K, bm), lambda i: (0, i)),
                   pl.BlockSpec((2 * TOP_K, bm), lambda i: (0, i)),
                   pl.BlockSpec((n_exp, LANES), lambda i: (0, 0))],
        out_shape=[jax.ShapeDtypeStruct((m, d // 2), U32),
                   jax.ShapeDtypeStruct((2 * TOP_K, m), I32),
                   jax.ShapeDtypeStruct((2 * TOP_K, m), F32),
                   jax.ShapeDtypeStruct((n_exp, LANES), F32)],
        scratch_shapes=[pltpu.VMEM((n_exp, LANES), F32)],
        compiler_params=_cparams(("arbitrary",)),
        name="router",
    )(x1, shift, scale, g2, w_rt_b, b_r, tri)


def _row_copy(src, s_row, dst, d_row, sem):
    return pltpu.make_async_copy(src.at[pl.ds(s_row, 1)], dst.at[pl.ds(d_row, 1)], sem)


ISSUE_UNROLL = 8
COMBINE_ROWS = 32


def _dispatch_kernel(pos_ref, h_ref, xs_ref, pos_s, sem_p, sem):
    bd = h_ref.shape[0]
    cp = pltpu.make_async_copy(pos_ref, pos_s, sem_p)
    cp.start()
    cp.wait()

    def issue(tb, c):
        for k in range(ISSUE_UNROLL):
            t = tb * ISSUE_UNROLL + k
            for s in range(TOP_K):
                _row_copy(h_ref, t, xs_ref, pos_s[s, t], sem).start()
        return c

    lax.fori_loop(0, bd // ISSUE_UNROLL, issue, 0)
    for s in range(TOP_K):
        pltpu.make_async_copy(h_ref, xs_ref.at[pl.ds(0, bd)], sem).wait()


def _dispatch(pos8, hp, *, rows, bd):
    m, dh = hp.shape
    return pl.pallas_call(
        _dispatch_kernel,
        grid=(m // bd,),
        in_specs=[pl.BlockSpec((2 * TOP_K, bd), lambda i: (0, i)),
                  pl.BlockSpec((bd, dh), lambda i: (i, 0))],
        out_specs=pl.BlockSpec(memory_space=pl.ANY),
        out_shape=jax.ShapeDtypeStruct((rows, dh), U32),
        scratch_shapes=[pltpu.SMEM((2 * TOP_K, bd), I32), pltpu.SemaphoreType.DMA(()),
                        pltpu.SemaphoreType.DMA(())],
        compiler_params=_cparams(("arbitrary",)),
        name="dispatch",
    )(pos8, hp)


N_ITEM_ARRAYS = 9


MOE_SUB_ROWS = 256


def _for_sub_tiles(valid, nrows, block_rows, fn):
    rows = min(MOE_SUB_ROWS, block_rows)
    for r0 in range(0, block_rows, rows):
        @pl.when(valid & (nrows > r0))
        def _():
            fn(r0, rows)


def _group_weights(l, ie, ic, if_ref, ihn, ine, inc, hbm_refs, stg, dsts, sem, width):
    def copies(e, c):
        col = pl.multiple_of(c * width, width)
        return [pltpu.make_async_copy(h.at[e, :, pl.ds(col, width)], stg.at[k], sem.at[k])
                for k, h in enumerate(hbm_refs)]

    @pl.when(l == 0)
    def _():
        for cp in copies(ie[0], ic[0]):
            cp.start()

    @pl.when(if_ref[l] == 1)
    def _():
        for cp in copies(ie[l], ic[l]):
            cp.wait()
        for k, dst in enumerate(dsts):
            dst[...] = stg[k].astype(BF16)

        @pl.when(ihn[l] == 1)
        def _():
            for cp in copies(ine[l], inc[l]):
                cp.start()


def _moe_up_kernel(ie, ic, it, if_ref, iv_ref, in_ref, ihn, ine, inc,
                   xs_ref, wg_hbm, wu_hbm, bg_ref, bu_ref, a_ref, stg, wg_s, wu_s, sem):
    l = pl.program_id(0)
    _group_weights(l, ie, ic, if_ref, ihn, ine, inc, (wg_hbm, wu_hbm), stg, (wg_s, wu_s), sem, wg_s.shape[1])

    def sub_tile(r0, rows):
        w = xs_ref[r0:r0 + rows, :]
        row = r0 + lax.broadcasted_iota(I32, w.shape, 0)
        w = jnp.where(row < in_ref[l], w, jnp.uint32(0))
        lo, hi = _unpack_pair(w)
        x = jnp.concatenate([lo.astype(BF16), hi.astype(BF16)], axis=1)
        gate = jnp.minimum(jnp.dot(x, wg_s[...], preferred_element_type=F32) + bg_ref[...], SWIGLU_LIMIT)
        lin = jnp.clip(jnp.dot(x, wu_s[...], preferred_element_type=F32) + bu_ref[...],
                       -SWIGLU_LIMIT, SWIGLU_LIMIT)
        a_ref[r0:r0 + rows, :] = (gate * jax.nn.sigmoid(SWIGLU_ALPHA * gate) * (lin + 1.0)).astype(BF16)

    _for_sub_tiles(iv_ref[l] == 1, in_ref[l], xs_ref.shape[0], sub_tile)


def _moe_up(items, xs, w_gate, w_up, b_gate, b_up, *, bmoe, nck):
    rows, dh = xs.shape
    n_exp, d, ff = w_gate.shape
    fc = ff // nck
    n_items = items[0].shape[0]
    grid_spec = pltpu.PrefetchScalarGridSpec(
        num_scalar_prefetch=N_ITEM_ARRAYS,
        grid=(n_items,),
        in_specs=[pl.BlockSpec((bmoe, dh), lambda l, ie, ic, it, *_: (it[l], 0)),
                  pl.BlockSpec(memory_space=pl.ANY),
                  pl.BlockSpec(memory_space=pl.ANY),
                  pl.BlockSpec((None, 1, fc), lambda l, ie, ic, it, *_: (ie[l], 0, ic[l])),
                  pl.BlockSpec((None, 1, fc), lambda l, ie, ic, it, *_: (ie[l], 0, ic[l]))],
        out_specs=pl.BlockSpec((bmoe, fc), lambda l, ie, ic, it, *_: (it[l], ic[l])),
        scratch_shapes=[pltpu.VMEM((2, d, fc), F32), pltpu.VMEM((d, fc), BF16), pltpu.VMEM((d, fc), BF16),
                        pltpu.SemaphoreType.DMA((2,))],
    )
    return pl.pallas_call(
        _moe_up_kernel,
        grid_spec=grid_spec,
        out_shape=jax.ShapeDtypeStruct((rows, ff), BF16),
        compiler_params=_cparams(("arbitrary",)),
        name="moe_up",
    )(*items, xs, w_gate, w_up, b_gate, b_up)


def _moe_down_kernel(ie, ic, it, if_ref, iv_ref, in_ref, ihn, ine, inc,
                     a_ref, wd_hbm, bd_ref, y_ref, stg, wd_s, sem):
    l = pl.program_id(0)
    _group_weights(l, ie, ic, if_ref, ihn, ine, inc, (wd_hbm,), stg, (wd_s,), sem, wd_s.shape[1])

    def sub_tile(r0, rows):
        y = jnp.dot(a_ref[r0:r0 + rows, :], wd_s[...], preferred_element_type=F32) + bd_ref[...]
        half = y.shape[1] // 2
        y_ref[r0:r0 + rows, :] = _pack_pair(y[:, :half], y[:, half:])

    _for_sub_tiles(iv_ref[l] == 1, in_ref[l], a_ref.shape[0], sub_tile)


def _moe_down(items, act, w_down, b_down, *, bmoe, nck):
    rows, ff = act.shape
    n_exp, _, d = w_down.shape
    dc = d // nck
    n_items = items[0].shape[0]
    grid_spec = pltpu.PrefetchScalarGridSpec(
        num_scalar_prefetch=N_ITEM_ARRAYS,
        grid=(n_items,),
        in_specs=[pl.BlockSpec((bmoe, ff), lambda l, ie, ic, it, *_: (it[l], 0)),
                  pl.BlockSpec(memory_space=pl.ANY),
                  pl.BlockSpec((None, 1, dc), lambda l, ie, ic, it, *_: (ie[l], 0, ic[l]))],
        out_specs=pl.BlockSpec((bmoe, dc // 2), lambda l, ie, ic, it, *_: (it[l], ic[l])),
        scratch_shapes=[pltpu.VMEM((1, ff, dc), F32), pltpu.VMEM((ff, dc), BF16), pltpu.SemaphoreType.DMA((1,))],
    )
    return pl.pallas_call(
        _moe_down_kernel,
        grid_spec=grid_spec,
        out_shape=jax.ShapeDtypeStruct((rows, d // 2), U32),
        compiler_params=_cparams(("arbitrary",)),
        name="moe_down",
    )(*items, act, w_down, b_down)


def _combine_kernel(pos_ref, posn_ref, w_ref, x_ref, g_ref, ys_ref, o_ref, pos_s, buf, sem_p, sem, *, nck):
    i = pl.program_id(0)
    bc = x_ref.shape[0]
    slot = i % 2
    rc = COMBINE_ROWS

    def load_pos(p_ref):
        cp = pltpu.make_async_copy(p_ref, pos_s, sem_p)
        cp.start()
        cp.wait()

    def issue_rows(t0, n, sl):
        for k in range(n):
            for s in range(TOP_K):
                _row_copy(ys_ref, pos_s[s, t0 + k], buf.at[sl, s], t0 + k, sem.at[sl]).start()

    def wait_slot(sl):
        for s in range(TOP_K):
            pltpu.make_async_copy(ys_ref.at[pl.ds(0, bc)], buf.at[sl, s], sem.at[sl]).wait()

    @pl.when(i == 0)
    def _():
        load_pos(pos_ref)

        def issue(tb, c):
            issue_rows(tb * ISSUE_UNROLL, ISSUE_UNROLL, 0)
            return c

        lax.fori_loop(0, bc // ISSUE_UNROLL, issue, 0)

    load_pos(posn_ref)
    dq = x_ref.shape[1] // (2 * nck)

    def step(cur):
        wait_slot(cur)

        def rows_body(rb, carry):
            r0 = pl.multiple_of(rb * rc, rc)
            issue_rows(r0, rc, 1 - cur)
            wrow = w_ref[pl.ds(r0, rc), :]
            ws = [jnp.broadcast_to(wrow[:, s:s + 1], (rc, LANES)) for s in range(TOP_K)]
            for k0 in range(0, nck * dq, LANES):
                lo_acc = hi_acc = None
                for s in range(TOP_K):
                    lo, hi = _unpack_pair(buf[cur, s, pl.ds(r0, rc), k0:k0 + LANES])
                    lo_acc = ws[s] * lo if lo_acc is None else lo_acc + ws[s] * lo
                    hi_acc = ws[s] * hi if hi_acc is None else hi_acc + ws[s] * hi
                c = k0 // dq
                col = c * 2 * dq + (k0 - c * dq)
                for cc, acc in ((col, lo_acc), (col + dq, hi_acc)):
                    cs = slice(cc, cc + LANES)
                    o_ref[pl.ds(r0, rc), cs] = x_ref[pl.ds(r0, rc), cs] + g_ref[:, cs] * acc
            return carry

        lax.fori_loop(0, bc // rc, rows_body, 0)

        @pl.when(i == pl.num_programs(0) - 1)
        def _():
            wait_slot(1 - cur)

    for cur in range(2):
        pl.when(slot == cur)(functools.partial(step, cur))


def _combine(pos8, wtok, x1, gate, ys, *, seq, nck, bc):
    m, d = x1.shape
    tpb = seq // bc
    nsteps = m // bc
    kern = functools.partial(_combine_kernel, nck=nck)
    return pl.pallas_call(
        kern,
        grid=(nsteps,),
        in_specs=[pl.BlockSpec((2 * TOP_K, bc), lambda i: (0, i)),
                  pl.BlockSpec((2 * TOP_K, bc), lambda i: (0, jnp.minimum(i + 1, nsteps - 1))),
                  pl.BlockSpec((bc, 2 * TOP_K), lambda i: (i, 0)),
                  pl.BlockSpec((bc, d), lambda i: (i, 0)),
                  pl.BlockSpec((None, 1, d), lambda i: (i // tpb, 0, 0)),
                  pl.BlockSpec(memory_space=pl.ANY)],
        out_specs=pl.BlockSpec((bc, d), lambda i: (i, 0)),
        out_shape=jax.ShapeDtypeStruct((m, d), F32),
        scratch_shapes=[pltpu.SMEM((2 * TOP_K, bc), I32), pltpu.VMEM((2, TOP_K, bc, d // 2), U32),
                        pltpu.SemaphoreType.DMA(()), pltpu.SemaphoreType.DMA((2,))],
        compiler_params=_cparams(("arbitrary",)),
        name="combine",
    )(pos8, pos8, wtok, x1, gate, ys)


def _rope_tables(seq):
    t = jnp.arange(seq)
    row = (t // GRID_W).astype(F32)
    col = (t % GRID_W).astype(F32)
    inv_freq = ROPE_BASE ** (-jnp.arange(0, ROPE_HALF, 2, dtype=F32) / ROPE_HALF)
    ar = row[:, None] * inv_freq[None, :]
    ac = col[:, None] * inv_freq[None, :]
    cos = jnp.concatenate([jnp.cos(ar), jnp.cos(ar), jnp.cos(ac), jnp.cos(ac)], axis=1)
    sin = jnp.concatenate([-jnp.sin(ar), jnp.sin(ar), -jnp.sin(ac), jnp.sin(ac)], axis=1)
    rep = LANES // HEAD_DIM
    return jnp.tile(cos, (1, rep)), jnp.tile(sin, (1, rep))


def _window_bias(seq):
    i = jnp.arange(BLOCK)[:, None]
    j = jnp.arange(3 * BLOCK)[None, :]
    band = (j >= i) & (j <= i + 2 * WINDOW)
    first = band & (j >= BLOCK)
    last = band & (j < 2 * BLOCK)
    both = first & last
    variants = jnp.stack([both if seq == BLOCK else first, band, last])
    return jnp.where(variants, 0.0, NEG_INF).astype(F32)


def _work_items(counts, *, bmoe, nck, n_tiles):
    n_exp = counts.shape[0]
    tiles_e = (counts + bmoe - 1) // bmoe
    tile_end = jnp.cumsum(tiles_e)
    tile_start = tile_end - tiles_e
    item_end = nck * tile_end
    total = item_end[-1]
    l = jnp.arange(nck * n_tiles, dtype=I32)
    valid = l < total
    lc = jnp.minimum(l, total - 1)
    e = jnp.minimum(jnp.sum((item_end[None, :] <= lc[:, None]).astype(I32), axis=1), n_exp - 1)
    is_e = e[:, None] == jnp.arange(n_exp, dtype=I32)[None, :]
    of_e = lambda v: jnp.sum(jnp.where(is_e, v[None, :], 0), axis=1)
    nt = jnp.maximum(of_e(tiles_e), 1)
    ts = of_e(tile_start)
    within = lc - nck * ts
    c = within // nt
    k = within % nt
    t = ts + k
    first = (k == 0) & valid
    nrows = jnp.clip(of_e(counts) - k * bmoe, 0, bmoe)
    e_after = jnp.sum((item_end[None, :] <= of_e(item_end)[:, None]).astype(I32), axis=1)
    same_e = c + 1 < nck
    has_next = valid & (same_e | (e_after < n_exp))
    next_e = jnp.where(same_e, e, jnp.minimum(e_after, n_exp - 1))
    next_c = jnp.where(same_e, c + 1, 0)
    as_i = lambda a: a.astype(I32)
    return (as_i(e), as_i(c), as_i(t), as_i(first), as_i(valid), as_i(nrows),
            as_i(has_next), as_i(next_e), as_i(next_c)), tile_start * bmoe


def kernel(x, c, ctx, c_ctx, w_ada, b_ada, norm1_g, w_in, q_norm_g, k_norm_g, sinks, w_pool, pool_scale,
           w_out, norm2_g, w_router, b_router, w_gate, b_gate, w_up, b_up, w_down, b_down):
    assert w_ada.shape[0] == 1, "one layer: the context stream is only read, never updated"
    batch, seq, d = x.shape
    ctx_len = ctx.shape[1]
    aw = d // 2
    nq = aw // HEAD_DIM
    nkv = max(1, nq // 8)
    grp = nq // nkv
    kvw = nkv * HEAD_DIM
    pw = d - aw
    n_exp, _, ff = w_gate.shape[1:]
    m = batch * seq
    assert seq % BLOCK == 0 and seq % GRID_W == 0 and kvw % LANES == 0 and grp % 2 == 0 and aw == pw

    c8 = jnp.zeros((8, d), F32).at[:batch].set(c).at[batch].set(c_ctx)
    mod = _ada(c8, w_ada[0], b_ada)
    mod_b = mod[:batch].reshape(batch, N_MOD, 1, d)
    mod_c = mod[batch].reshape(N_MOD, 1, d)

    w_in_b = w_in[0].astype(BF16)
    w_out_b = w_out[0].astype(BF16)
    cos_t, sin_t = _rope_tables(seq)
    qg = jnp.tile(q_norm_g[0].astype(F32), LANES // HEAD_DIM)[None, :]
    kg = jnp.tile(k_norm_g[0].astype(F32), LANES // HEAD_DIM)[None, :]
    swap16 = lambda g: g.reshape(LANES // ROPE_HALF, 2, ROPE_HALF // 2)[:, ::-1, :].reshape(1, LANES)
    qs = HEAD_DIM ** -0.5
    qa, qb = cos_t * (qg * qs), sin_t * (swap16(qg) * qs)
    ka, kb = cos_t * kg, sin_t * swap16(kg)
    bn = 2 * kvw
    lane_head = jnp.arange(bn) // HEAD_DIM
    bd = jnp.where(lane_head[:, None] == lane_head[None, :], 1.0 / HEAD_DIM, 0.0).astype(BF16)
    g1 = norm1_g[0][None, :]

    x2 = x.reshape(m, d)
    q, kv, u = _inproj(x2, mod_b[:, 0], mod_b[:, 1], g1, w_in_b, qa, qb, ka, kb, bd,
                       seq=seq, aw=aw, kvw=kvw, pw=pw)
    kvc = _ctx_kv(ctx.reshape(batch * ctx_len, d), mod_c[0], mod_c[1], g1,
                  w_in_b[:, aw:aw + bn], kg, bd, kvw=kvw)
    attn = _attn(q, kv, kvc, _window_bias(seq), sinks[0].astype(F32),
                 batch=batch, seq=seq, ctx_len=ctx_len, nkv=nkv, grp=grp, kvw=kvw)
    pool = _pool(u, w_pool[0].astype(BF16), pool_scale, seq=seq)
    x1 = _outproj(attn, pool, w_out_b, x2, mod_b[:, 2], seq=seq)

    bm5 = min(512, seq)
    tri = (jnp.arange(bm5)[:, None] < jnp.arange(bm5)[None, :]).astype(BF16)
    hp, meta_i, meta_w, cnt = _router(x1, mod_b[:, 3], mod_b[:, 4], norm2_g[0][None, :],
                                      w_router[0].T.astype(BF16), b_router[0][:, None], tri, seq=seq)
    counts = cnt[:, 0].astype(I32)
    bmoe = 512 if m * TOP_K >= 512 * n_exp else 128
    nck_up, nck_dn = 2, 1
    n_tiles = (m * TOP_K) // bmoe + n_exp
    items_up, row_start = _work_items(counts, bmoe=bmoe, nck=nck_up, n_tiles=n_tiles)
    items_dn, _ = _work_items(counts, bmoe=bmoe, nck=nck_dn, n_tiles=n_tiles)
    idx, rank = meta_i[:TOP_K], meta_i[TOP_K:]
    start_of = jnp.sum(jnp.where(idx[..., None] == jnp.arange(n_exp), row_start, 0), axis=-1)
    pos8 = jnp.concatenate([start_of + rank, jnp.zeros_like(rank)], axis=0).astype(I32)

    xs = _dispatch(pos8, hp, rows=n_tiles * bmoe, bd=min(256, seq))
    act = _moe_up(items_up, xs, w_gate[0], w_up[0], b_gate[0][:, None, :], b_up[0][:, None, :],
                  bmoe=bmoe, nck=nck_up)
    ys = _moe_down(items_dn, act, w_down[0], b_down[0][:, None, :], bmoe=bmoe, nck=nck_dn)
    out = _combine(pos8, meta_w.T, x1, mod_b[:, 5], ys, seq=seq, nck=nck_dn, bc=min(256, seq))
    return out.reshape(batch, seq, d)
```

```python
import functools

import jax
import jax.numpy as jnp
from jax import lax
from jax.experimental import pallas as pl
from jax.experimental.pallas import tpu as pltpu

F32 = jnp.float32
BF16 = jnp.bfloat16
U32 = jnp.uint32
I32 = jnp.int32

HEAD_DIM = 64
LANES = 128
GRID_W = 64
WINDOW = 128
BLOCK = 128
ROPE_HALF = HEAD_DIM // 2
ROPE_BASE = 10000.0
POOL_WINDOWS = (2, 4, 8, 16)
POOL_HALO = 8
TOP_K = 4
N_MOD = 6
NORM_EPS = 1e-6
NEG_INF = -1e30
SWIGLU_LIMIT = 7.0
SWIGLU_ALPHA = 1.702
VMEM_LIMIT = 56 * 1024 * 1024


def _cparams(sem):
    return pltpu.CompilerParams(dimension_semantics=sem, vmem_limit_bytes=VMEM_LIMIT)


def _pack_pair(lo, hi):
    a = pltpu.bitcast(lo.astype(BF16).astype(F32), U32) >> 16
    b = pltpu.bitcast(hi.astype(BF16).astype(F32), U32) & jnp.uint32(0xFFFF0000)
    return a | b


def _unpack_pair(w):
    lo = pltpu.bitcast(w << 16, F32)
    hi = pltpu.bitcast(w & jnp.uint32(0xFFFF0000), F32)
    return lo, hi


def _ada_kernel(c_ref, w_ref, b_ref, o_ref):
    c = c_ref[...]
    s = c * jax.nn.sigmoid(c)
    hi = s.astype(BF16)
    lo = (s - hi.astype(F32)).astype(BF16)
    lhs = jnp.concatenate([hi, lo], axis=0)
    r = jnp.dot(lhs, w_ref[...].astype(BF16), preferred_element_type=F32)
    o_ref[...] = r[:8] + r[8:] + b_ref[...]


def _ada(c8, w_ada, b_ada):
    d, n = w_ada.shape
    bn = min(512, n)
    return pl.pallas_call(
        _ada_kernel,
        grid=(n // bn,),
        in_specs=[pl.BlockSpec((8, d), lambda j: (0, 0)),
                  pl.BlockSpec((d, bn), lambda j: (0, j)),
                  pl.BlockSpec((1, bn), lambda j: (0, j))],
        out_specs=pl.BlockSpec((8, bn), lambda j: (0, j)),
        out_shape=jax.ShapeDtypeStruct((8, n), F32),
        compiler_params=_cparams(("parallel",)),
        name="ada",
    )(c8, w_ada, b_ada)


def _norm_mod(x, g, shift, scale):
    ms = jnp.mean(x * x, axis=-1, keepdims=True)
    y = x * lax.rsqrt(ms + NORM_EPS) * g
    return y * (1.0 + scale) + shift


def _head_rnorm(z, bd):
    msq = jnp.dot((z * z).astype(BF16), bd, preferred_element_type=F32)
    return lax.rsqrt(msq + NORM_EPS)


def _rope(z, ta, tb):
    rows = z.shape[0]
    lane = lax.broadcasted_iota(I32, (rows, LANES), 1)
    first = (lane % ROPE_HALF) < (ROPE_HALF // 2)
    outs = []
    for s in range(z.shape[1] // LANES):
        zs = z[:, s * LANES:(s + 1) * LANES]
        up = pltpu.roll(zs, LANES - ROPE_HALF // 2, 1)
        dn = pltpu.roll(zs, ROPE_HALF // 2, 1)
        outs.append(zs * ta + jnp.where(first, up, dn) * tb)
    return outs[0] if len(outs) == 1 else jnp.concatenate(outs, axis=1)


def _tile_gain(g_ref, width):
    g = g_ref[...]
    return g if width == LANES else jnp.concatenate([g] * (width // LANES), axis=1)


INPROJ_NORM_ROWS = 256


def _inproj_kernel(x_hbm, sh_ref, sc_ref, g1_ref, w_ref, qa_ref, qb_ref, ka_ref, kb_ref, bd_ref,
                   q_ref, kv_ref, u_ref, xbuf, h_ref, zs_ref, sem, *, nq_t, kvw):
    i = pl.program_id(0)
    j = pl.program_id(1)
    cur = j % 2
    bm = xbuf.shape[0]

    def mm():
        return jnp.dot(h_ref[...], w_ref[...], preferred_element_type=F32)

    def x_copy(rb):
        return pltpu.make_async_copy(x_hbm.at[pl.ds(pl.multiple_of(rb * bm, bm), bm)], xbuf, sem)

    @pl.when((i == 0) & (j == 0))
    def _():
        x_copy(0).start()

    @pl.when(j == 0)
    def _():
        x_copy(i).wait()
        rows = min(INPROJ_NORM_ROWS, bm)
        for r in range(0, bm, rows):
            h_ref[r:r + rows, :] = _norm_mod(xbuf[r:r + rows, :], g1_ref[...], sh_ref[...],
                                             sc_ref[...]).astype(BF16)

        @pl.when(i + 1 < pl.num_programs(0))
        def _():
            x_copy(i + 1).start()

        zs_ref[0] = mm()

    @pl.when((j >= 1) & (j <= nq_t))
    def _():
        zp = zs_ref[1 - cur]
        zs_ref[cur] = mm()
        q_ref[...] = (_rope(zp, qa_ref[...], qb_ref[...]) * _head_rnorm(zp, bd_ref[...])).astype(BF16)

    @pl.when(j == nq_t + 1)
    def _():
        zp = zs_ref[1 - cur]
        u_ref[...] = mm()
        zk = zp[:, :kvw]
        kv_ref[:, :kvw] = (_rope(zk, ka_ref[...], kb_ref[...]) * _head_rnorm(zk, bd_ref[:kvw, :kvw])).astype(BF16)
        kv_ref[:, kvw:] = zp[:, kvw:].astype(BF16)

    @pl.when(j > nq_t + 1)
    def _():
        u_ref[...] = mm()


def _inproj(x2, shift, scale, g1, w_in_b, qa, qb, ka, kb, bd, *, seq, aw, kvw, pw):
    m, d = x2.shape
    bn = 2 * kvw
    nq_t, nu_t = aw // bn, pw // bn
    bm = min(1024, seq)
    tpb = seq // bm
    kern = functools.partial(_inproj_kernel, nq_t=nq_t, kvw=kvw)
    return pl.pallas_call(
        kern,
        grid=(m // bm, nq_t + 1 + nu_t),
        in_specs=[pl.BlockSpec(memory_space=pl.ANY),
                  pl.BlockSpec((None, 1, d), lambda i, j: (i // tpb, 0, 0)),
                  pl.BlockSpec((None, 1, d), lambda i, j: (i // tpb, 0, 0)),
                  pl.BlockSpec((1, d), lambda i, j: (0, 0)),
                  pl.BlockSpec((d, bn), lambda i, j: (0, j)),
                  pl.BlockSpec((bm, LANES), lambda i, j: (i % tpb, 0)),
                  pl.BlockSpec((bm, LANES), lambda i, j: (i % tpb, 0)),
                  pl.BlockSpec((bm, LANES), lambda i, j: (i % tpb, 0)),
                  pl.BlockSpec((bm, LANES), lambda i, j: (i % tpb, 0)),
                  pl.BlockSpec((bn, bn), lambda i, j: (0, 0))],
        out_specs=[pl.BlockSpec((bm, bn), lambda i, j: (i, jnp.clip(j - 1, 0, nq_t - 1))),
                   pl.BlockSpec((bm, bn), lambda i, j: (i, 0)),
                   pl.BlockSpec((bm, bn), lambda i, j: (i, jnp.clip(j - nq_t - 1, 0, nu_t - 1)))],
        out_shape=[jax.ShapeDtypeStruct((m, aw), BF16),
                   jax.ShapeDtypeStruct((m, 2 * kvw), BF16),
                   jax.ShapeDtypeStruct((m, pw), F32)],
        scratch_shapes=[pltpu.VMEM((bm, d), F32), pltpu.VMEM((bm, d), BF16), pltpu.VMEM((2, bm, bn), F32),
                        pltpu.SemaphoreType.DMA(())],
        compiler_params=_cparams(("arbitrary", "arbitrary")),
        name="inproj",
    )(x2, shift, scale, g1, w_in_b, qa, qb, ka, kb, bd)


def _ctx_kv_kernel(x_ref, sh_ref, sc_ref, g1_ref, w_ref, kg_ref, bd_ref, kv_ref, *, kvw):
    h = _norm_mod(x_ref[...], g1_ref[...], sh_ref[...], sc_ref[...]).astype(BF16)
    z = jnp.dot(h, w_ref[...], preferred_element_type=F32)
    zk = z[:, :kvw]
    kv_ref[:, :kvw] = (zk * _head_rnorm(zk, bd_ref[:kvw, :kvw]) * _tile_gain(kg_ref, kvw)).astype(BF16)
    kv_ref[:, kvw:] = z[:, kvw:].astype(BF16)


def _ctx_kv(c2, shift, scale, g1, w_kv_b, kg, bd, *, kvw):
    m, d = c2.shape
    bn = 2 * kvw
    bm = min(256, m)
    kern = functools.partial(_ctx_kv_kernel, kvw=kvw)
    return pl.pallas_call(
        kern,
        grid=(m // bm,),
        in_specs=[pl.BlockSpec((bm, d), lambda i: (i, 0)),
                  pl.BlockSpec((1, d), lambda i: (0, 0)),
                  pl.BlockSpec((1, d), lambda i: (0, 0)),
                  pl.BlockSpec((1, d), lambda i: (0, 0)),
                  pl.BlockSpec((d, bn), lambda i: (0, 0)),
                  pl.BlockSpec((1, LANES), lambda i: (0, 0)),
                  pl.BlockSpec((bn, bn), lambda i: (0, 0))],
        out_specs=pl.BlockSpec((bm, bn), lambda i: (i, 0)),
        out_shape=jax.ShapeDtypeStruct((m, bn), BF16),
        compiler_params=_cparams(("parallel",)),
        name="ctx_kv",
    )(c2, shift, scale, g1, w_kv_b, kg, bd)


def _attn_kernel(sink_ref, q_ref, kp_ref, kc_ref, kn_ref, kx_ref, bias_ref, o_ref, *, nkv, grp, kvw, ctx_len):
    npair = grp // 2
    nk = ctx_len + 3 * BLOCK
    lane_kv = lax.broadcasted_iota(I32, (nk, LANES), 1)
    lane_q = lax.broadcasted_iota(I32, (BLOCK, LANES), 1)
    bias = bias_ref[...]
    for h in range(nkv):
        slab = (h // 2) * LANES
        mine = (lane_kv // HEAD_DIM) == (h % 2)

        def window(off):
            w = jnp.concatenate([kx_ref[:, off + slab:off + slab + LANES],
                                 kp_ref[:, off + slab:off + slab + LANES],
                                 kc_ref[:, off + slab:off + slab + LANES],
                                 kn_ref[:, off + slab:off + slab + LANES]], axis=0).astype(F32)
            wz = jnp.where(mine, w, 0.0)
            wr = pltpu.roll(wz, HEAD_DIM, 1)
            lo, hi = (wz, wr) if h % 2 == 0 else (wr, wz)
            return lo.astype(BF16), hi.astype(BF16)

        k_e, k_o = window(0)
        v_e, v_o = window(kvw)
        q0 = h * grp * HEAD_DIM
        q4 = jnp.concatenate([q_ref[:, q0 + p * LANES:q0 + (p + 1) * LANES] for p in range(npair)], axis=0)
        kcat = jnp.concatenate([k_e, k_o], axis=0)
        s_all = lax.dot_general(q4, kcat, (((1,), (1,)), ((), ())), preferred_element_type=F32)
        p_e, p_o, rinv = [], [], []
        for p in range(npair):
            halves = []
            for half in range(2):
                sink = sink_ref[h * grp + 2 * p + half]
                s = s_all[p * BLOCK:(p + 1) * BLOCK, half * nk:(half + 1) * nk]
                s = jnp.concatenate([s[:, :ctx_len], s[:, ctx_len:] + bias], axis=1)
                mx = jnp.maximum(jnp.max(s, axis=-1, keepdims=True), sink)
                e = jnp.exp(s - mx)
                den = jnp.sum(e, axis=-1, keepdims=True) + jnp.exp(sink - mx)
                halves.append((e.astype(BF16), 1.0 / den))
            p_e.append(halves[0][0])
            p_o.append(halves[1][0])
            rinv.append(jnp.where(lane_q < HEAD_DIM, halves[0][1], halves[1][1]))
        o = (jnp.dot(jnp.concatenate(p_e, axis=0), v_e, preferred_element_type=F32)
             + jnp.dot(jnp.concatenate(p_o, axis=0), v_o, preferred_element_type=F32))
        for p in range(npair):
            o_ref[:, q0 + p * LANES:q0 + (p + 1) * LANES] = (o[p * BLOCK:(p + 1) * BLOCK] * rinv[p]).astype(BF16)


def _attn(q, kv, kvc, bias, sinks, *, batch, seq, ctx_len, nkv, grp, kvw):
    m, aw = q.shape
    nb = seq // BLOCK
    kern = functools.partial(_attn_kernel, nkv=nkv, grp=grp, kvw=kvw, ctx_len=ctx_len)

    def sel(n):
        first = 1 - jnp.minimum(n, 1)
        last = 1 - jnp.minimum(nb - 1 - n, 1)
        return 1 - first + last

    return pl.pallas_call(
        kern,
        grid=(batch, nb),
        in_specs=[pl.BlockSpec(memory_space=pltpu.SMEM),
                  pl.BlockSpec((BLOCK, aw), lambda b, n: (b * nb + n, 0)),
                  pl.BlockSpec((BLOCK, 2 * kvw), lambda b, n: (b * nb + jnp.maximum(n - 1, 0), 0)),
                  pl.BlockSpec((BLOCK, 2 * kvw), lambda b, n: (b * nb + n, 0)),
                  pl.BlockSpec((BLOCK, 2 * kvw), lambda b, n: (b * nb + jnp.minimum(n + 1, nb - 1), 0)),
                  pl.BlockSpec((ctx_len, 2 * kvw), lambda b, n: (b, 0)),
                  pl.BlockSpec((None, BLOCK, 3 * BLOCK), lambda b, n: (sel(n), 0, 0))],
        out_specs=pl.BlockSpec((BLOCK, aw), lambda b, n: (b * nb + n, 0)),
        out_shape=jax.ShapeDtypeStruct((m, aw), BF16),
        compiler_params=_cparams(("parallel", "parallel")),
        name="attn",
    )(sinks, q, kv, kv, kv, kvc, bias)


def _pool_kernel(up_ref, uc_ref, un_ref, wp_ref, ps_ref, o_ref, *, seq, pgw):
    bm = uc_ref.shape[0]
    t0 = (pl.program_id(0) * bm) % seq
    rows = bm + 2 * POOL_HALO
    pt = t0 + lax.broadcasted_iota(I32, (bm, rows), 0)
    ps = t0 - POOL_HALO + lax.broadcasted_iota(I32, (bm, rows), 1)
    pt1 = t0 + lax.broadcasted_iota(I32, (bm, 1), 0)
    for g, w in enumerate(POOL_WINDOWS):
        cs = slice(g * pgw, (g + 1) * pgw)
        uc = uc_ref[:, cs]
        ucat = jnp.concatenate([up_ref[:, cs], uc, un_ref[:, cs]], axis=0)
        lo = jnp.clip(pt - w // 2, 0, seq)
        hi = jnp.clip(pt + w // 2, 0, seq)
        band = jnp.where((ps >= lo) & (ps < hi), 1.0, 0.0).astype(BF16)
        cnt = (jnp.clip(pt1 + w // 2, 0, seq) - jnp.clip(pt1 - w // 2, 0, seq)).astype(F32)
        tot = jnp.dot(band, ucat.astype(BF16), preferred_element_type=F32)
        dlt = tot / cnt - uc
        y = jnp.dot(dlt.astype(BF16), wp_ref[g], preferred_element_type=F32)
        o_ref[:, cs] = (y * ps_ref[:, cs]).astype(BF16)


def _pool(u, w_pool_b, pool_scale, *, seq):
    m, pw = u.shape
    pgw = pw // len(POOL_WINDOWS)
    bm = min(256, seq)
    hb = bm // POOL_HALO
    nh = m // POOL_HALO
    kern = functools.partial(_pool_kernel, seq=seq, pgw=pgw)
    return pl.pallas_call(
        kern,
        grid=(m // bm,),
        in_specs=[pl.BlockSpec((POOL_HALO, pw), lambda i: (jnp.maximum(i * hb - 1, 0), 0)),
                  pl.BlockSpec((bm, pw), lambda i: (i, 0)),
                  pl.BlockSpec((POOL_HALO, pw), lambda i: (jnp.minimum((i + 1) * hb, nh - 1), 0)),
                  pl.BlockSpec((len(POOL_WINDOWS), pgw, pgw), lambda i: (0, 0, 0)),
                  pl.BlockSpec((1, pw), lambda i: (0, 0))],
        out_specs=pl.BlockSpec((bm, pw), lambda i: (i, 0)),
        out_shape=jax.ShapeDtypeStruct((m, pw), BF16),
        compiler_params=_cparams(("parallel",)),
        name="pool",
    )(u, u, u, w_pool_b, pool_scale)


def _outproj_kernel(a_ref, p_ref, wa_ref, wp_ref, x_ref, g_ref, o_ref):
    mix = (jnp.dot(a_ref[...], wa_ref[...], preferred_element_type=F32)
           + jnp.dot(p_ref[...], wp_ref[...], preferred_element_type=F32))
    o_ref[...] = x_ref[...] + g_ref[...] * mix


def _outproj(attn, pool, w_out_b, x2, gate, *, seq):
    m, d = x2.shape
    aw = attn.shape[1]
    bm = min(1024, seq)
    bn = min(1024, d)
    tpb = seq // bm
    return pl.pallas_call(
        _outproj_kernel,
        grid=(m // bm, d // bn),
        in_specs=[pl.BlockSpec((bm, aw), lambda i, j: (i, 0)),
                  pl.BlockSpec((bm, aw), lambda i, j: (i, 0)),
                  pl.BlockSpec((aw, bn), lambda i, j: (0, j)),
                  pl.BlockSpec((aw, bn), lambda i, j: (1, j)),
                  pl.BlockSpec((bm, bn), lambda i, j: (i, j)),
                  pl.BlockSpec((None, 1, bn), lambda i, j: (i // tpb, 0, j))],
        out_specs=pl.BlockSpec((bm, bn), lambda i, j: (i, j)),
        out_shape=jax.ShapeDtypeStruct((m, d), F32),
        compiler_params=_cparams(("parallel", "parallel")),
        name="outproj",
    )(attn, pool, w_out_b, w_out_b, x2, gate)


def _router_kernel(x_ref, sh_ref, sc_ref, g2_ref, wr_ref, br_ref, tri_ref,
                   hp_ref, mi_ref, mw_ref, cnt_ref, carry_ref, *, n_exp):
    i = pl.program_id(0)

    @pl.when(i == 0)
    def _():
        carry_ref[...] = jnp.zeros_like(carry_ref)

    h = _norm_mod(x_ref[...], g2_ref[...], sh_ref[...], sc_ref[...])
    bm, d = h.shape
    hp_ref[...] = _pack_pair(h[:, :d // 2], h[:, d // 2:])
    logits = lax.dot_general(wr_ref[...], h.astype(BF16), (((1,), (1,)), ((), ())),
                             preferred_element_type=F32) + br_ref[...]
    eidx = lax.broadcasted_iota(I32, (n_exp, bm), 0)
    l = logits
    tops, idxs, sels = [], [], []
    for _ in range(TOP_K):
        mx = jnp.max(l, axis=0, keepdims=True)
        ix = jnp.min(jnp.where(l == mx, eidx, n_exp), axis=0, keepdims=True)
        sel = eidx == ix
        tops.append(mx)
        idxs.append(ix)
        sels.append(sel)
        l = jnp.where(sel, -jnp.inf, l)
    es = [jnp.exp(t - tops[0]) for t in tops]
    den = es[0] + es[1] + es[2] + es[3]
    member = jnp.where(sels[0] | sels[1] | sels[2] | sels[3], 1.0, 0.0)
    before = jnp.dot(member.astype(BF16), tri_ref[...], preferred_element_type=F32) + carry_ref[:, 0:1]
    ranks = [jnp.sum(jnp.where(s, before, 0.0), axis=0, keepdims=True).astype(I32) for s in sels]
    carry_ref[...] = carry_ref[...] + jnp.sum(member, axis=1, keepdims=True)
    mi_ref[...] = jnp.concatenate(idxs + ranks, axis=0)
    mw_ref[...] = jnp.concatenate([e / den for e in es] + [jnp.zeros((TOP_K, bm), F32)], axis=0)
    cnt_ref[...] = carry_ref[...]


def _router(x1, shift, scale, g2, w_rt_b, b_r, tri, *, seq):
    m, d = x1.shape
    n_exp = w_rt_b.shape[0]
    bm = tri.shape[0]
    tpb = seq // bm
    kern = functools.partial(_router_kernel, n_exp=n_exp)
    return pl.pallas_call(
        kern,
        grid=(m // bm,),
        in_specs=[pl.BlockSpec((bm, d), lambda i: (i, 0)),
                  pl.BlockSpec((None, 1, d), lambda i: (i // tpb, 0, 0)),
                  pl.BlockSpec((None, 1, d), lambda i: (i // tpb, 0, 0)),
                  pl.BlockSpec((1, d), lambda i: (0, 0)),
                  pl.BlockSpec((n_exp, d), lambda i: (0, 0)),
                  pl.BlockSpec((n_exp, 1), lambda i: (0, 0)),
                  pl.BlockSpec((bm, bm), lambda i: (0, 0))],
        out_specs=[pl.BlockSpec((bm, d // 2), lambda i: (i, 0)),
                   pl.BlockSpec((2 * TOP_K, bm), lambda i: (0, i)),
                   pl.BlockSpec((2 * TOP_K, bm), lambda i: (0, i)),
                   pl.BlockSpec((n_exp, LANES), lambda i: (0, 0))],
        out_shape=[jax.ShapeDtypeStruct((m, d // 2), U32),
                   jax.ShapeDtypeStruct((2 * TOP_K, m), I32),
                   jax.ShapeDtypeStruct((2 * TOP_K, m), F32),
                   jax.ShapeDtypeStruct((n_exp, LANES), F32)],
        scratch_shapes=[pltpu.VMEM((n_exp, LANES), F32)],
        compiler_params=_cparams(("arbitrary",)),
        name="router",
    )(x1, shift, scale, g2, w_rt_b, b_r, tri)


SUBLANES = 8
COMBINE_ROWS = 32


def _sorted_row(ref3, pos_s, s, t):
    return ref3.at[pos_s[s, t], pl.ds(pos_s[TOP_K + s, t], 1)]


def _dispatch_kernel(pos_ref, h_ref, xs_ref, pos_s, sem_p, sem):
    bd = h_ref.shape[0]
    cp = pltpu.make_async_copy(pos_ref, pos_s, sem_p)
    cp.start()
    cp.wait()
    copies = [pltpu.make_async_copy(h_ref.at[pl.ds(t, 1)], _sorted_row(xs_ref, pos_s, s, t), sem)
              for t in range(bd) for s in range(TOP_K)]
    for cp in copies:
        cp.start()
    for cp in copies:
        cp.wait()


def _dispatch(pos8, hp, *, rows, bd):
    m, dh = hp.shape
    return pl.pallas_call(
        _dispatch_kernel,
        grid=(m // bd,),
        in_specs=[pl.BlockSpec((2 * TOP_K, bd), lambda i: (0, i)),
                  pl.BlockSpec((bd, dh), lambda i: (i, 0))],
        out_specs=pl.BlockSpec(memory_space=pl.ANY),
        out_shape=jax.ShapeDtypeStruct((rows // SUBLANES, SUBLANES, dh), U32),
        scratch_shapes=[pltpu.SMEM((2 * TOP_K, bd), I32), pltpu.SemaphoreType.DMA(()),
                        pltpu.SemaphoreType.DMA(())],
        compiler_params=_cparams(("arbitrary",)),
        name="dispatch",
    )(pos8, hp)


N_ITEM_ARRAYS = 9


MOE_SUB_ROWS = 256


def _for_sub_tiles(valid, nrows, block_rows, fn):
    rows = min(MOE_SUB_ROWS, block_rows)
    for r0 in range(0, block_rows, rows):
        @pl.when(valid & (nrows > r0))
        def _():
            fn(r0, rows)


def _group_weights(l, ie, ic, if_ref, ihn, ine, inc, hbm_refs, stg, dsts, sem, width):
    def copies(e, c):
        col = pl.multiple_of(c * width, width)
        return [pltpu.make_async_copy(h.at[e, :, pl.ds(col, width)], stg.at[k], sem.at[k])
                for k, h in enumerate(hbm_refs)]

    @pl.when(l == 0)
    def _():
        for cp in copies(ie[0], ic[0]):
            cp.start()

    @pl.when(if_ref[l] == 1)
    def _():
        for cp in copies(ie[l], ic[l]):
            cp.wait()
        for k, dst in enumerate(dsts):
            dst[...] = stg[k].astype(BF16)

        @pl.when(ihn[l] == 1)
        def _():
            for cp in copies(ine[l], inc[l]):
                cp.start()


def _moe_up_kernel(ie, ic, it, if_ref, iv_ref, in_ref, ihn, ine, inc,
                   xs_ref, wg_hbm, wu_hbm, bg_ref, bu_ref, a_ref, stg, wg_s, wu_s, sem):
    l = pl.program_id(0)
    _group_weights(l, ie, ic, if_ref, ihn, ine, inc, (wg_hbm, wu_hbm), stg, (wg_s, wu_s), sem, wg_s.shape[1])

    def sub_tile(r0, rows):
        w = xs_ref[r0:r0 + rows, :]
        row = r0 + lax.broadcasted_iota(I32, w.shape, 0)
        w = jnp.where(row < in_ref[l], w, jnp.uint32(0))
        lo, hi = _unpack_pair(w)
        x = jnp.concatenate([lo.astype(BF16), hi.astype(BF16)], axis=1)
        gate = jnp.minimum(jnp.dot(x, wg_s[...], preferred_element_type=F32) + bg_ref[...], SWIGLU_LIMIT)
        lin = jnp.clip(jnp.dot(x, wu_s[...], preferred_element_type=F32) + bu_ref[...],
                       -SWIGLU_LIMIT, SWIGLU_LIMIT)
        a_ref[r0:r0 + rows, :] = (gate * jax.nn.sigmoid(SWIGLU_ALPHA * gate) * (lin + 1.0)).astype(BF16)

    _for_sub_tiles(iv_ref[l] == 1, in_ref[l], xs_ref.shape[0], sub_tile)


def _moe_up(items, xs, w_gate, w_up, b_gate, b_up, *, bmoe, nck):
    rows, dh = xs.shape
    n_exp, d, ff = w_gate.shape
    fc = ff // nck
    n_items = items[0].shape[0]
    grid_spec = pltpu.PrefetchScalarGridSpec(
        num_scalar_prefetch=N_ITEM_ARRAYS,
        grid=(n_items,),
        in_specs=[pl.BlockSpec((bmoe, dh), lambda l, ie, ic, it, *_: (it[l], 0)),
                  pl.BlockSpec(memory_space=pl.ANY),
                  pl.BlockSpec(memory_space=pl.ANY),
                  pl.BlockSpec((None, 1, fc), lambda l, ie, ic, it, *_: (ie[l], 0, ic[l])),
                  pl.BlockSpec((None, 1, fc), lambda l, ie, ic, it, *_: (ie[l], 0, ic[l]))],
        out_specs=pl.BlockSpec((bmoe, fc), lambda l, ie, ic, it, *_: (it[l], ic[l])),
        scratch_shapes=[pltpu.VMEM((2, d, fc), F32), pltpu.VMEM((d, fc), BF16), pltpu.VMEM((d, fc), BF16),
                        pltpu.SemaphoreType.DMA((2,))],
    )
    return pl.pallas_call(
        _moe_up_kernel,
        grid_spec=grid_spec,
        out_shape=jax.ShapeDtypeStruct((rows, ff), BF16),
        compiler_params=_cparams(("arbitrary",)),
        name="moe_up",
    )(*items, xs, w_gate, w_up, b_gate, b_up)


def _moe_down_kernel(ie, ic, it, if_ref, iv_ref, in_ref, ihn, ine, inc,
                     a_ref, wd_hbm, bd_ref, y_ref, stg, wd_s, sem):
    l = pl.program_id(0)
    _group_weights(l, ie, ic, if_ref, ihn, ine, inc, (wd_hbm,), stg, (wd_s,), sem, wd_s.shape[1])

    def sub_tile(r0, rows):
        y = jnp.dot(a_ref[r0:r0 + rows, :], wd_s[...], preferred_element_type=F32) + bd_ref[...]
        half = y.shape[1] // 2
        y_ref[r0:r0 + rows, :] = _pack_pair(y[:, :half], y[:, half:])

    _for_sub_tiles(iv_ref[l] == 1, in_ref[l], a_ref.shape[0], sub_tile)


def _moe_down(items, act, w_down, b_down, *, bmoe, nck):
    rows, ff = act.shape
    n_exp, _, d = w_down.shape
    dc = d // nck
    n_items = items[0].shape[0]
    grid_spec = pltpu.PrefetchScalarGridSpec(
        num_scalar_prefetch=N_ITEM_ARRAYS,
        grid=(n_items,),
        in_specs=[pl.BlockSpec((bmoe, ff), lambda l, ie, ic, it, *_: (it[l], 0)),
                  pl.BlockSpec(memory_space=pl.ANY),
                  pl.BlockSpec((None, 1, dc), lambda l, ie, ic, it, *_: (ie[l], 0, ic[l]))],
        out_specs=pl.BlockSpec((bmoe, dc // 2), lambda l, ie, ic, it, *_: (it[l], ic[l])),
        scratch_shapes=[pltpu.VMEM((1, ff, dc), F32), pltpu.VMEM((ff, dc), BF16), pltpu.SemaphoreType.DMA((1,))],
    )
    return pl.pallas_call(
        _moe_down_kernel,
        grid_spec=grid_spec,
        out_shape=jax.ShapeDtypeStruct((rows, d // 2), U32),
        compiler_params=_cparams(("arbitrary",)),
        name="moe_down",
    )(*items, act, w_down, b_down)


def _combine_kernel(pos_ref, posn_ref, w_ref, x_ref, g_ref, ys_ref, o_ref, pos_s, buf, sem_p, sem, *, nck):
    i = pl.program_id(0)
    bc = x_ref.shape[0]
    slot = i % 2
    rc = COMBINE_ROWS

    def load_pos(p_ref):
        cp = pltpu.make_async_copy(p_ref, pos_s, sem_p)
        cp.start()
        cp.wait()

    def issue_rows(t0, n, sl):
        for t in range(t0, t0 + n):
            for s in range(TOP_K):
                dst = buf.at[sl, s, t // SUBLANES, pl.ds(t % SUBLANES, 1)]
                pltpu.make_async_copy(_sorted_row(ys_ref, pos_s, s, t), dst, sem.at[sl]).start()

    def wait_slot(sl):
        for s in range(TOP_K):
            pltpu.make_async_copy(ys_ref.at[pl.ds(0, bc // SUBLANES)], buf.at[sl, s], sem.at[sl]).wait()

    @pl.when(i == 0)
    def _():
        load_pos(pos_ref)
        issue_rows(0, bc, 0)

    load_pos(posn_ref)
    dq = x_ref.shape[1] // (2 * nck)

    def step(cur):
        wait_slot(cur)
        for r0 in range(0, bc, rc):
            issue_rows(r0, rc, 1 - cur)
            wrow = w_ref[r0:r0 + rc, :]
            ws = [jnp.broadcast_to(wrow[:, s:s + 1], (rc, LANES)) for s in range(TOP_K)]
            tiles = slice(r0 // SUBLANES, (r0 + rc) // SUBLANES)
            for k0 in range(0, nck * dq, LANES):
                lo_acc = hi_acc = None
                for s in range(TOP_K):
                    lo, hi = _unpack_pair(buf[cur, s, tiles, :, k0:k0 + LANES].reshape(rc, LANES))
                    lo_acc = ws[s] * lo if lo_acc is None else lo_acc + ws[s] * lo
                    hi_acc = ws[s] * hi if hi_acc is None else hi_acc + ws[s] * hi
                c = k0 // dq
                col = c * 2 * dq + (k0 - c * dq)
                for cc, acc in ((col, lo_acc), (col + dq, hi_acc)):
                    cs = slice(cc, cc + LANES)
                    o_ref[r0:r0 + rc, cs] = x_ref[r0:r0 + rc, cs] + g_ref[:, cs] * acc

        @pl.when(i == pl.num_programs(0) - 1)
        def _():
            wait_slot(1 - cur)

    for cur in range(2):
        pl.when(slot == cur)(functools.partial(step, cur))


def _combine(pos8, wtok, x1, gate, ys, *, seq, nck, bc):
    m, d = x1.shape
    tpb = seq // bc
    nsteps = m // bc
    kern = functools.partial(_combine_kernel, nck=nck)
    return pl.pallas_call(
        kern,
        grid=(nsteps,),
        in_specs=[pl.BlockSpec((2 * TOP_K, bc), lambda i: (0, i)),
                  pl.BlockSpec((2 * TOP_K, bc), lambda i: (0, jnp.minimum(i + 1, nsteps - 1))),
                  pl.BlockSpec((bc, 2 * TOP_K), lambda i: (i, 0)),
                  pl.BlockSpec((bc, d), lambda i: (i, 0)),
                  pl.BlockSpec((None, 1, d), lambda i: (i // tpb, 0, 0)),
                  pl.BlockSpec(memory_space=pl.ANY)],
        out_specs=pl.BlockSpec((bc, d), lambda i: (i, 0)),
        out_shape=jax.ShapeDtypeStruct((m, d), F32),
        scratch_shapes=[pltpu.SMEM((2 * TOP_K, bc), I32),
                        pltpu.VMEM((2, TOP_K, bc // SUBLANES, SUBLANES, d // 2), U32),
                        pltpu.SemaphoreType.DMA(()), pltpu.SemaphoreType.DMA((2,))],
        compiler_params=_cparams(("arbitrary",)),
        name="combine",
    )(pos8, pos8, wtok, x1, gate, ys)


def _rope_tables(seq):
    t = jnp.arange(seq)
    row = (t // GRID_W).astype(F32)
    col = (t % GRID_W).astype(F32)
    inv_freq = ROPE_BASE ** (-jnp.arange(0, ROPE_HALF, 2, dtype=F32) / ROPE_HALF)
    ar = row[:, None] * inv_freq[None, :]
    ac = col[:, None] * inv_freq[None, :]
    cos = jnp.concatenate([jnp.cos(ar), jnp.cos(ar), jnp.cos(ac), jnp.cos(ac)], axis=1)
    sin = jnp.concatenate([-jnp.sin(ar), jnp.sin(ar), -jnp.sin(ac), jnp.sin(ac)], axis=1)
    rep = LANES // HEAD_DIM
    return jnp.tile(cos, (1, rep)), jnp.tile(sin, (1, rep))


def _window_bias(seq):
    i = jnp.arange(BLOCK)[:, None]
    j = jnp.arange(3 * BLOCK)[None, :]
    band = (j >= i) & (j <= i + 2 * WINDOW)
    first = band & (j >= BLOCK)
    last = band & (j < 2 * BLOCK)
    both = first & last
    variants = jnp.stack([both if seq == BLOCK else first, band, last])
    return jnp.where(variants, 0.0, NEG_INF).astype(F32)


def _work_items(counts, *, bmoe, nck, n_tiles):
    n_exp = counts.shape[0]
    tiles_e = (counts + bmoe - 1) // bmoe
    tile_end = jnp.cumsum(tiles_e)
    tile_start = tile_end - tiles_e
    item_end = nck * tile_end
    total = item_end[-1]
    l = jnp.arange(nck * n_tiles, dtype=I32)
    valid = l < total
    lc = jnp.minimum(l, total - 1)
    e = jnp.minimum(jnp.sum((item_end[None, :] <= lc[:, None]).astype(I32), axis=1), n_exp - 1)
    is_e = e[:, None] == jnp.arange(n_exp, dtype=I32)[None, :]
    of_e = lambda v: jnp.sum(jnp.where(is_e, v[None, :], 0), axis=1)
    nt = jnp.maximum(of_e(tiles_e), 1)
    ts = of_e(tile_start)
    within = lc - nck * ts
    c = within // nt
    k = within % nt
    t = ts + k
    first = (k == 0) & valid
    nrows = jnp.clip(of_e(counts) - k * bmoe, 0, bmoe)
    e_after = jnp.sum((item_end[None, :] <= of_e(item_end)[:, None]).astype(I32), axis=1)
    same_e = c + 1 < nck
    has_next = valid & (same_e | (e_after < n_exp))
    next_e = jnp.where(same_e, e, jnp.minimum(e_after, n_exp - 1))
    next_c = jnp.where(same_e, c + 1, 0)
    as_i = lambda a: a.astype(I32)
    return (as_i(e), as_i(c), as_i(t), as_i(first), as_i(valid), as_i(nrows),
            as_i(has_next), as_i(next_e), as_i(next_c)), tile_start * bmoe


def kernel(x, c, ctx, c_ctx, w_ada, b_ada, norm1_g, w_in, q_norm_g, k_norm_g, sinks, w_pool, pool_scale,
           w_out, norm2_g, w_router, b_router, w_gate, b_gate, w_up, b_up, w_down, b_down):
    assert w_ada.shape[0] == 1, "one layer: the context stream is only read, never updated"
    batch, seq, d = x.shape
    ctx_len = ctx.shape[1]
    aw = d // 2
    nq = aw // HEAD_DIM
    nkv = max(1, nq // 8)
    grp = nq // nkv
    kvw = nkv * HEAD_DIM
    pw = d - aw
    n_exp, _, ff = w_gate.shape[1:]
    m = batch * seq
    assert seq % BLOCK == 0 and seq % GRID_W == 0 and kvw % LANES == 0 and grp % 2 == 0 and aw == pw

    c8 = jnp.zeros((8, d), F32).at[:batch].set(c).at[batch].set(c_ctx)
    mod = _ada(c8, w_ada[0], b_ada)
    mod_b = mod[:batch].reshape(batch, N_MOD, 1, d)
    mod_c = mod[batch].reshape(N_MOD, 1, d)

    w_in_b = w_in[0].astype(BF16)
    w_out_b = w_out[0].astype(BF16)
    cos_t, sin_t = _rope_tables(seq)
    qg = jnp.tile(q_norm_g[0].astype(F32), LANES // HEAD_DIM)[None, :]
    kg = jnp.tile(k_norm_g[0].astype(F32), LANES // HEAD_DIM)[None, :]
    swap16 = lambda g: g.reshape(LANES // ROPE_HALF, 2, ROPE_HALF // 2)[:, ::-1, :].reshape(1, LANES)
    qs = HEAD_DIM ** -0.5
    qa, qb = cos_t * (qg * qs), sin_t * (swap16(qg) * qs)
    ka, kb = cos_t * kg, sin_t * swap16(kg)
    bn = 2 * kvw
    lane_head = jnp.arange(bn) // HEAD_DIM
    bd = jnp.where(lane_head[:, None] == lane_head[None, :], 1.0 / HEAD_DIM, 0.0).astype(BF16)
    g1 = norm1_g[0][None, :]

    x2 = x.reshape(m, d)
    q, kv, u = _inproj(x2, mod_b[:, 0], mod_b[:, 1], g1, w_in_b, qa, qb, ka, kb, bd,
                       seq=seq, aw=aw, kvw=kvw, pw=pw)
    kvc = _ctx_kv(ctx.reshape(batch * ctx_len, d), mod_c[0], mod_c[1], g1,
                  w_in_b[:, aw:aw + bn], kg, bd, kvw=kvw)
    attn = _attn(q, kv, kvc, _window_bias(seq), sinks[0].astype(F32),
                 batch=batch, seq=seq, ctx_len=ctx_len, nkv=nkv, grp=grp, kvw=kvw)
    pool = _pool(u, w_pool[0].astype(BF16), pool_scale, seq=seq)
    x1 = _outproj(attn, pool, w_out_b, x2, mod_b[:, 2], seq=seq)

    bm5 = min(512, seq)
    tri = (jnp.arange(bm5)[:, None] < jnp.arange(bm5)[None, :]).astype(BF16)
    hp, meta_i, meta_w, cnt = _router(x1, mod_b[:, 3], mod_b[:, 4], norm2_g[0][None, :],
                                      w_router[0].T.astype(BF16), b_router[0][:, None], tri, seq=seq)
    counts = cnt[:, 0].astype(I32)
    bmoe = 1024 if m * TOP_K >= 1024 * n_exp else 128
    nck_up, nck_dn = 2, 1
    n_tiles = (m * TOP_K) // bmoe + n_exp
    items_up, row_start = _work_items(counts, bmoe=bmoe, nck=nck_up, n_tiles=n_tiles)
    items_dn, _ = _work_items(counts, bmoe=bmoe, nck=nck_dn, n_tiles=n_tiles)
    idx, rank = meta_i[:TOP_K], meta_i[TOP_K:]
    start_of = jnp.sum(jnp.where(idx[..., None] == jnp.arange(n_exp), row_start, 0), axis=-1)
    pos = start_of + rank
    pos8 = jnp.concatenate([pos // SUBLANES, pos % SUBLANES], axis=0).astype(I32)

    rows = n_tiles * bmoe
    xs = _dispatch(pos8, hp, rows=rows, bd=min(256, seq)).reshape(rows, d // 2)
    act = _moe_up(items_up, xs, w_gate[0], w_up[0], b_gate[0][:, None, :], b_up[0][:, None, :],
                  bmoe=bmoe, nck=nck_up)
    ys = _moe_down(items_dn, act, w_down[0], b_down[0][:, None, :], bmoe=bmoe, nck=nck_dn)
    out = _combine(pos8, meta_w.T, x1, mod_b[:, 5], ys.reshape(rows // SUBLANES, SUBLANES, d // 2),
                   seq=seq, nck=nck_dn, bc=min(256, seq))
    return out.reshape(batch, seq, d)
```

```python
import functools

import jax
import jax.numpy as jnp
from jax import lax
from jax.experimental import pallas as pl
from jax.experimental.pallas import tpu as pltpu

F32 = jnp.float32
BF16 = jnp.bfloat16
U32 = jnp.uint32
I32 = jnp.int32

HEAD_DIM = 64
LANES = 128
GRID_W = 64
WINDOW = 128
BLOCK = 128
ROPE_HALF = HEAD_DIM // 2
ROPE_BASE = 10000.0
POOL_WINDOWS = (2, 4, 8, 16)
POOL_HALO = 8
TOP_K = 4
N_MOD = 6
NORM_EPS = 1e-6
NEG_INF = -1e30
SWIGLU_LIMIT = 7.0
SWIGLU_ALPHA = 1.702
LOG2E = 1.4426950408889634
VMEM_LIMIT = 56 * 1024 * 1024


def _cparams(sem):
    return pltpu.CompilerParams(dimension_semantics=sem, vmem_limit_bytes=VMEM_LIMIT)


def _pack_pair(lo, hi):
    a = pltpu.bitcast(lo.astype(BF16).astype(F32), U32) >> 16
    b = pltpu.bitcast(hi.astype(BF16).astype(F32), U32) & jnp.uint32(0xFFFF0000)
    return a | b


def _unpack_pair(w):
    lo = pltpu.bitcast(w << 16, F32)
    hi = pltpu.bitcast(w & jnp.uint32(0xFFFF0000), F32)
    return lo, hi


def _ada_kernel(c_ref, w_ref, b_ref, o_ref):
    c = c_ref[...]
    s = c * jax.nn.sigmoid(c)
    hi = s.astype(BF16)
    lo = (s - hi.astype(F32)).astype(BF16)
    lhs = jnp.concatenate([hi, lo], axis=0)
    r = jnp.dot(lhs, w_ref[...].astype(BF16), preferred_element_type=F32)
    o_ref[...] = r[:8] + r[8:] + b_ref[...]


def _ada(c8, w_ada, b_ada):
    d, n = w_ada.shape
    bn = min(512, n)
    return pl.pallas_call(
        _ada_kernel,
        grid=(n // bn,),
        in_specs=[pl.BlockSpec((8, d), lambda j: (0, 0)),
                  pl.BlockSpec((d, bn), lambda j: (0, j)),
                  pl.BlockSpec((1, bn), lambda j: (0, j))],
        out_specs=pl.BlockSpec((8, bn), lambda j: (0, j)),
        out_shape=jax.ShapeDtypeStruct((8, n), F32),
        compiler_params=_cparams(("parallel",)),
        name="ada",
    )(c8, w_ada, b_ada)


def _norm_mod(x, g, shift, scale):
    ms = jnp.mean(x * x, axis=-1, keepdims=True)
    y = x * lax.rsqrt(ms + NORM_EPS) * g
    return y * (1.0 + scale) + shift


def _head_rnorm(z, bd):
    msq = jnp.dot((z * z).astype(BF16), bd, preferred_element_type=F32)
    return lax.rsqrt(msq + NORM_EPS)


def _rope(z, ta, tb):
    rows = z.shape[0]
    lane = lax.broadcasted_iota(I32, (rows, LANES), 1)
    first = (lane % ROPE_HALF) < (ROPE_HALF // 2)
    outs = []
    for s in range(z.shape[1] // LANES):
        zs = z[:, s * LANES:(s + 1) * LANES]
        up = pltpu.roll(zs, LANES - ROPE_HALF // 2, 1)
        dn = pltpu.roll(zs, ROPE_HALF // 2, 1)
        outs.append(zs * ta + jnp.where(first, up, dn) * tb)
    return outs[0] if len(outs) == 1 else jnp.concatenate(outs, axis=1)


def _tile_gain(g_ref, width):
    g = g_ref[...]
    return g if width == LANES else jnp.concatenate([g] * (width // LANES), axis=1)


INPROJ_NORM_ROWS = 256


def _inproj_kernel(x_hbm, sh_ref, sc_ref, g1_ref, w_ref, qa_ref, qb_ref, ka_ref, kb_ref, bd_ref,
                   q_ref, kv_ref, u_ref, xbuf, h_ref, zs_ref, sem, *, nq_t, kvw):
    i = pl.program_id(0)
    j = pl.program_id(1)
    cur = j % 2
    bm = xbuf.shape[0]

    def mm():
        return jnp.dot(h_ref[...], w_ref[...], preferred_element_type=F32)

    def x_copy(rb):
        return pltpu.make_async_copy(x_hbm.at[pl.ds(pl.multiple_of(rb * bm, bm), bm)], xbuf, sem)

    @pl.when((i == 0) & (j == 0))
    def _():
        x_copy(0).start()

    @pl.when(j == 0)
    def _():
        x_copy(i).wait()
        rows = min(INPROJ_NORM_ROWS, bm)
        for r in range(0, bm, rows):
            h_ref[r:r + rows, :] = _norm_mod(xbuf[r:r + rows, :], g1_ref[...], sh_ref[...],
                                             sc_ref[...]).astype(BF16)

        @pl.when(i + 1 < pl.num_programs(0))
        def _():
            x_copy(i + 1).start()

        zs_ref[0] = mm()

    @pl.when((j >= 1) & (j <= nq_t))
    def _():
        zp = zs_ref[1 - cur]
        zs_ref[cur] = mm()
        q_ref[...] = (_rope(zp, qa_ref[...], qb_ref[...]) * _head_rnorm(zp, bd_ref[...])).astype(BF16)

    @pl.when(j == nq_t + 1)
    def _():
        zp = zs_ref[1 - cur]
        u_ref[...] = mm()
        zk = zp[:, :kvw]
        kv_ref[:, :kvw] = (_rope(zk, ka_ref[...], kb_ref[...]) * _head_rnorm(zk, bd_ref[:kvw, :kvw])).astype(BF16)
        kv_ref[:, kvw:] = zp[:, kvw:].astype(BF16)

    @pl.when(j > nq_t + 1)
    def _():
        u_ref[...] = mm()


def _inproj(x2, shift, scale, g1, w_in_b, qa, qb, ka, kb, bd, *, seq, aw, kvw, pw):
    m, d = x2.shape
    bn = 2 * kvw
    nq_t, nu_t = aw // bn, pw // bn
    bm = min(1024, seq)
    tpb = seq // bm
    kern = functools.partial(_inproj_kernel, nq_t=nq_t, kvw=kvw)
    return pl.pallas_call(
        kern,
        grid=(m // bm, nq_t + 1 + nu_t),
        in_specs=[pl.BlockSpec(memory_space=pl.ANY),
                  pl.BlockSpec((None, 1, d), lambda i, j: (i // tpb, 0, 0)),
                  pl.BlockSpec((None, 1, d), lambda i, j: (i // tpb, 0, 0)),
                  pl.BlockSpec((1, d), lambda i, j: (0, 0)),
                  pl.BlockSpec((d, bn), lambda i, j: (0, j)),
                  pl.BlockSpec((bm, LANES), lambda i, j: (i % tpb, 0)),
                  pl.BlockSpec((bm, LANES), lambda i, j: (i % tpb, 0)),
                  pl.BlockSpec((bm, LANES), lambda i, j: (i % tpb, 0)),
                  pl.BlockSpec((bm, LANES), lambda i, j: (i % tpb, 0)),
                  pl.BlockSpec((bn, bn), lambda i, j: (0, 0))],
        out_specs=[pl.BlockSpec((bm, bn), lambda i, j: (i, jnp.clip(j - 1, 0, nq_t - 1))),
                   pl.BlockSpec((bm, bn), lambda i, j: (i, 0)),
                   pl.BlockSpec((bm, bn), lambda i, j: (i, jnp.clip(j - nq_t - 1, 0, nu_t - 1)))],
        out_shape=[jax.ShapeDtypeStruct((m, aw), BF16),
                   jax.ShapeDtypeStruct((m, 2 * kvw), BF16),
                   jax.ShapeDtypeStruct((m, pw), F32)],
        scratch_shapes=[pltpu.VMEM((bm, d), F32), pltpu.VMEM((bm, d), BF16), pltpu.VMEM((2, bm, bn), F32),
                        pltpu.SemaphoreType.DMA(())],
        compiler_params=_cparams(("arbitrary", "arbitrary")),
        name="inproj",
    )(x2, shift, scale, g1, w_in_b, qa, qb, ka, kb, bd)


def _ctx_kv_kernel(x_ref, sh_ref, sc_ref, g1_ref, w_ref, kg_ref, bd_ref, kv_ref, *, kvw):
    h = _norm_mod(x_ref[...], g1_ref[...], sh_ref[...], sc_ref[...]).astype(BF16)
    z = jnp.dot(h, w_ref[...], preferred_element_type=F32)
    zk = z[:, :kvw]
    kv_ref[:, :kvw] = (zk * _head_rnorm(zk, bd_ref[:kvw, :kvw]) * _tile_gain(kg_ref, kvw)).astype(BF16)
    kv_ref[:, kvw:] = z[:, kvw:].astype(BF16)


def _ctx_kv(c2, shift, scale, g1, w_kv_b, kg, bd, *, kvw):
    m, d = c2.shape
    bn = 2 * kvw
    bm = min(256, m)
    kern = functools.partial(_ctx_kv_kernel, kvw=kvw)
    return pl.pallas_call(
        kern,
        grid=(m // bm,),
        in_specs=[pl.BlockSpec((bm, d), lambda i: (i, 0)),
                  pl.BlockSpec((1, d), lambda i: (0, 0)),
                  pl.BlockSpec((1, d), lambda i: (0, 0)),
                  pl.BlockSpec((1, d), lambda i: (0, 0)),
                  pl.BlockSpec((d, bn), lambda i: (0, 0)),
                  pl.BlockSpec((1, LANES), lambda i: (0, 0)),
                  pl.BlockSpec((bn, bn), lambda i: (0, 0))],
        out_specs=pl.BlockSpec((bm, bn), lambda i: (i, 0)),
        out_shape=jax.ShapeDtypeStruct((m, bn), BF16),
        compiler_params=_cparams(("parallel",)),
        name="ctx_kv",
    )(c2, shift, scale, g1, w_kv_b, kg, bd)


def _attn_kernel(sink_ref, q_ref, kp_ref, kc_ref, kn_ref, kx_ref, bias_ref, o_ref, *, nkv, grp, kvw, ctx_len):
    npair = grp // 2
    nk = ctx_len + 3 * BLOCK
    nq = npair * BLOCK
    lane_kv = lax.broadcasted_iota(I32, (nk, LANES), 1)
    row_o = lax.broadcasted_iota(I32, (LANES, nq), 0)
    bias = bias_ref[...]
    for h in range(nkv):
        slab = (h // 2) * LANES
        mine = (lane_kv // HEAD_DIM) == (h % 2)

        def window(off):
            w = jnp.concatenate([kx_ref[:, off + slab:off + slab + LANES],
                                 kp_ref[:, off + slab:off + slab + LANES],
                                 kc_ref[:, off + slab:off + slab + LANES],
                                 kn_ref[:, off + slab:off + slab + LANES]], axis=0).astype(F32)
            wz = jnp.where(mine, w, 0.0)
            wr = pltpu.roll(wz, HEAD_DIM, 1)
            lo, hi = (wz, wr) if h % 2 == 0 else (wr, wz)
            return lo.astype(BF16), hi.astype(BF16)

        k_e, k_o = window(0)
        v_e, v_o = window(kvw)
        q0 = h * grp * HEAD_DIM
        q4 = jnp.concatenate([q_ref[:, q0 + p * LANES:q0 + (p + 1) * LANES] for p in range(npair)], axis=0)
        kcat = jnp.concatenate([k_e, k_o], axis=0)
        s_all = lax.dot_general(kcat, q4, (((1,), (1,)), ((), ())), preferred_element_type=F32)
        probs, rinv = [], []
        for half in range(2):
            sink = sink_ref[2 * h + half]
            s = s_all[half * nk:(half + 1) * nk, :]
            own = ctx_len + BLOCK
            s = jnp.concatenate([s[:ctx_len], s[ctx_len:own] + bias[:BLOCK], s[own:own + BLOCK],
                                 s[own + BLOCK:] + bias[2 * BLOCK:]], axis=0)
            mx = jnp.maximum(jnp.max(s, axis=0, keepdims=True), sink)
            e = jnp.exp2(s - mx)
            den = jnp.sum(e, axis=0, keepdims=True) + jnp.exp2(sink - mx)
            probs.append(e.astype(BF16))
            rinv.append(1.0 / den)
        tn = (((0,), (0,)), ((), ()))
        o_t = (lax.dot_general(v_e, probs[0], tn, preferred_element_type=F32)
               + lax.dot_general(v_o, probs[1], tn, preferred_element_type=F32))
        o = (o_t * jnp.where(row_o < HEAD_DIM, rinv[0], rinv[1])).T
        for p in range(npair):
            o_ref[:, q0 + p * LANES:q0 + (p + 1) * LANES] = o[p * BLOCK:(p + 1) * BLOCK].astype(BF16)


def _attn(q, kv, kvc, bias, sinks, *, batch, seq, ctx_len, nkv, grp, kvw):
    m, aw = q.shape
    nb = seq // BLOCK
    kern = functools.partial(_attn_kernel, nkv=nkv, grp=grp, kvw=kvw, ctx_len=ctx_len)

    def sel(n):
        first = 1 - jnp.minimum(n, 1)
        last = 1 - jnp.minimum(nb - 1 - n, 1)
        return 1 - first + last

    npair = grp // 2
    nq = npair * BLOCK
    sink_rows = jnp.repeat((sinks * LOG2E).reshape(nkv, npair, 2).transpose(0, 2, 1), BLOCK,
                           axis=2).reshape(2 * nkv, 1, nq)
    bias_t = jnp.tile(bias.transpose(0, 2, 1), (1, 1, npair))
    return pl.pallas_call(
        kern,
        grid=(batch, nb),
        in_specs=[pl.BlockSpec((2 * nkv, 1, nq), lambda b, n: (0, 0, 0)),
                  pl.BlockSpec((BLOCK, aw), lambda b, n: (b * nb + n, 0)),
                  pl.BlockSpec((BLOCK, 2 * kvw), lambda b, n: (b * nb + jnp.maximum(n - 1, 0), 0)),
                  pl.BlockSpec((BLOCK, 2 * kvw), lambda b, n: (b * nb + n, 0)),
                  pl.BlockSpec((BLOCK, 2 * kvw), lambda b, n: (b * nb + jnp.minimum(n + 1, nb - 1), 0)),
                  pl.BlockSpec((ctx_len, 2 * kvw), lambda b, n: (b, 0)),
                  pl.BlockSpec((None, 3 * BLOCK, nq), lambda b, n: (sel(n), 0, 0))],
        out_specs=pl.BlockSpec((BLOCK, aw), lambda b, n: (b * nb + n, 0)),
        out_shape=jax.ShapeDtypeStruct((m, aw), BF16),
        compiler_params=_cparams(("parallel", "parallel")),
        name="attn",
    )(sink_rows, q, kv, kv, kv, kvc, bias_t)


def _pool_kernel(up_ref, uc_ref, un_ref, wp_ref, ps_ref, o_ref, *, seq, pgw):
    bm = uc_ref.shape[0]
    t0 = (pl.program_id(0) * bm) % seq
    rows = bm + 2 * POOL_HALO
    pt = t0 + lax.broadcasted_iota(I32, (bm, rows), 0)
    ps = t0 - POOL_HALO + lax.broadcasted_iota(I32, (bm, rows), 1)
    pt1 = t0 + lax.broadcasted_iota(I32, (bm, 1), 0)
    for g, w in enumerate(POOL_WINDOWS):
        cs = slice(g * pgw, (g + 1) * pgw)
        uc = uc_ref[:, cs]
        ucat = jnp.concatenate([up_ref[:, cs], uc, un_ref[:, cs]], axis=0)
        lo = jnp.clip(pt - w // 2, 0, seq)
        hi = jnp.clip(pt + w // 2, 0, seq)
        band = jnp.where((ps >= lo) & (ps < hi), 1.0, 0.0).astype(BF16)
        cnt = (jnp.clip(pt1 + w // 2, 0, seq) - jnp.clip(pt1 - w // 2, 0, seq)).astype(F32)
        tot = jnp.dot(band, ucat.astype(BF16), preferred_element_type=F32)
        dlt = tot / cnt - uc
        y = jnp.dot(dlt.astype(BF16), wp_ref[g], preferred_element_type=F32)
        o_ref[:, cs] = (y * ps_ref[:, cs]).astype(BF16)


def _pool(u, w_pool_b, pool_scale, *, seq):
    m, pw = u.shape
    pgw = pw // len(POOL_WINDOWS)
    bm = min(256, seq)
    hb = bm // POOL_HALO
    nh = m // POOL_HALO
    kern = functools.partial(_pool_kernel, seq=seq, pgw=pgw)
    return pl.pallas_call(
        kern,
        grid=(m // bm,),
        in_specs=[pl.BlockSpec((POOL_HALO, pw), lambda i: (jnp.maximum(i * hb - 1, 0), 0)),
                  pl.BlockSpec((bm, pw), lambda i: (i, 0)),
                  pl.BlockSpec((POOL_HALO, pw), lambda i: (jnp.minimum((i + 1) * hb, nh - 1), 0)),
                  pl.BlockSpec((len(POOL_WINDOWS), pgw, pgw), lambda i: (0, 0, 0)),
                  pl.BlockSpec((1, pw), lambda i: (0, 0))],
        out_specs=pl.BlockSpec((bm, pw), lambda i: (i, 0)),
        out_shape=jax.ShapeDtypeStruct((m, pw), BF16),
        compiler_params=_cparams(("parallel",)),
        name="pool",
    )(u, u, u, w_pool_b, pool_scale)


def _outproj_kernel(a_ref, p_ref, wa_ref, wp_ref, x_ref, g_ref, o_ref):
    mix = (jnp.dot(a_ref[...], wa_ref[...], preferred_element_type=F32)
           + jnp.dot(p_ref[...], wp_ref[...], preferred_element_type=F32))
    o_ref[...] = x_ref[...] + g_ref[...] * mix


def _outproj(attn, pool, w_out_b, x2, gate, *, seq):
    m, d = x2.shape
    aw = attn.shape[1]
    bm = min(1024, seq)
    bn = min(1024, d)
    tpb = seq // bm
    return pl.pallas_call(
        _outproj_kernel,
        grid=(m // bm, d // bn),
        in_specs=[pl.BlockSpec((bm, aw), lambda i, j: (i, 0)),
                  pl.BlockSpec((bm, aw), lambda i, j: (i, 0)),
                  pl.BlockSpec((aw, bn), lambda i, j: (0, j)),
                  pl.BlockSpec((aw, bn), lambda i, j: (1, j)),
                  pl.BlockSpec((bm, bn), lambda i, j: (i, j)),
                  pl.BlockSpec((None, 1, bn), lambda i, j: (i // tpb, 0, j))],
        out_specs=pl.BlockSpec((bm, bn), lambda i, j: (i, j)),
        out_shape=jax.ShapeDtypeStruct((m, d), F32),
        compiler_params=_cparams(("parallel", "parallel")),
        name="outproj",
    )(attn, pool, w_out_b, w_out_b, x2, gate)


def _router_kernel(x_ref, sh_ref, sc_ref, g2_ref, wr_ref, br_ref, tri_ref,
                   hp_ref, mi_ref, mw_ref, cnt_ref, carry_ref, *, n_exp):
    i = pl.program_id(0)

    @pl.when(i == 0)
    def _():
        carry_ref[...] = jnp.zeros_like(carry_ref)

    h = _norm_mod(x_ref[...], g2_ref[...], sh_ref[...], sc_ref[...])
    bm, d = h.shape
    hp_ref[...] = _pack_pair(h[:, :d // 2], h[:, d // 2:])
    logits = lax.dot_general(wr_ref[...], h.astype(BF16), (((1,), (1,)), ((), ())),
                             preferred_element_type=F32) + br_ref[...]
    eidx = lax.broadcasted_iota(I32, (n_exp, bm), 0)
    l = logits
    tops, idxs, sels = [], [], []
    for _ in range(TOP_K):
        mx = jnp.max(l, axis=0, keepdims=True)
        ix = jnp.min(jnp.where(l == mx, eidx, n_exp), axis=0, keepdims=True)
        sel = eidx == ix
        tops.append(mx)
        idxs.append(ix)
        sels.append(sel)
        l = jnp.where(sel, -jnp.inf, l)
    es = [jnp.exp(t - tops[0]) for t in tops]
    den = es[0] + es[1] + es[2] + es[3]
    member = jnp.where(sels[0] | sels[1] | sels[2] | sels[3], 1.0, 0.0)
    before = jnp.dot(member.astype(BF16), tri_ref[...], preferred_element_type=F32) + carry_ref[:, 0:1]
    ranks = [jnp.sum(jnp.where(s, before, 0.0), axis=0, keepdims=True).astype(I32) for s in sels]
    carry_ref[...] = carry_ref[...] + jnp.sum(member, axis=1, keepdims=True)
    mi_ref[...] = jnp.concatenate(idxs + ranks, axis=0)
    mw_ref[...] = jnp.concatenate([e / den for e in es] + [jnp.zeros((TOP_K, bm), F32)], axis=0)
    cnt_ref[...] = carry_ref[...]


def _router(x1, shift, scale, g2, w_rt_b, b_r, tri, *, seq):
    m, d = x1.shape
    n_exp = w_rt_b.shape[0]
    bm = tri.shape[0]
    tpb = seq // bm
    kern = functools.partial(_router_kernel, n_exp=n_exp)
    return pl.pallas_call(
        kern,
        grid=(m // bm,),
        in_specs=[pl.BlockSpec((bm, d), lambda i: (i, 0)),
                  pl.BlockSpec((None, 1, d), lambda i: (i // tpb, 0, 0)),
                  pl.BlockSpec((None, 1, d), lambda i: (i // tpb, 0, 0)),
                  pl.BlockSpec((1, d), lambda i: (0, 0)),
                  pl.BlockSpec((n_exp, d), lambda i: (0, 0)),
                  pl.BlockSpec((n_exp, 1), lambda i: (0, 0)),
                  pl.BlockSpec((bm, bm), lambda i: (0, 0))],
        out_specs=[pl.BlockSpec((bm, d // 2), lambda i: (i, 0)),
                   pl.BlockSpec((2 * TOP_K, bm), lambda i: (0, i)),
                   pl.BlockSpec((2 * TOP_K, bm), lambda i: (0, i)),
                   pl.BlockSpec((n_exp, LANES), lambda i: (0, 0))],
        out_shape=[jax.ShapeDtypeStruct((m, d // 2), U32),
                   jax.ShapeDtypeStruct((2 * TOP_K, m), I32),
                   jax.ShapeDtypeStruct((2 * TOP_K, m), F32),
                   jax.ShapeDtypeStruct((n_exp, LANES), F32)],
        scratch_shapes=[pltpu.VMEM((n_exp, LANES), F32)],
        compiler_params=_cparams(("arbitrary",)),
        name="router",
    )(x1, shift, scale, g2, w_rt_b, b_r, tri)


SUBLANES = 8
COMBINE_ROWS = 32


def _sorted_row(ref3, pos_s, s, t):
    return ref3.at[pos_s[s, t], pl.ds(pos_s[TOP_K + s, t], 1)]


def _dispatch_kernel(pos_ref, h_ref, xs_ref, pos_s, sem_p, sem):
    bd = h_ref.shape[0]
    cp = pltpu.make_async_copy(pos_ref, pos_s, sem_p)
    cp.start()
    cp.wait()
    copies = [pltpu.make_async_copy(h_ref.at[pl.ds(t, 1)], _sorted_row(xs_ref, pos_s, s, t), sem)
              for t in range(bd) for s in range(TOP_K)]
    for k, cp in enumerate(copies):
        cp.start(priority=k % 2)
    for cp in copies:
        cp.wait()


def _dispatch(pos8, hp, *, rows, bd):
    m, dh = hp.shape
    return pl.pallas_call(
        _dispatch_kernel,
        grid=(m // bd,),
        in_specs=[pl.BlockSpec((2 * TOP_K, bd), lambda i: (0, i)),
                  pl.BlockSpec((bd, dh), lambda i: (i, 0))],
        out_specs=pl.BlockSpec(memory_space=pl.ANY),
        out_shape=jax.ShapeDtypeStruct((rows // SUBLANES, SUBLANES, dh), U32),
        scratch_shapes=[pltpu.SMEM((2 * TOP_K, bd), I32), pltpu.SemaphoreType.DMA(()),
                        pltpu.SemaphoreType.DMA(())],
        compiler_params=_cparams(("arbitrary",)),
        name="dispatch",
    )(pos8, hp)


N_ITEM_ARRAYS = 9


MOE_SUB_ROWS = 256


def _for_sub_tiles(valid, nrows, block_rows, fn):
    rows = min(MOE_SUB_ROWS, block_rows)
    for r0 in range(0, block_rows, rows):
        @pl.when(valid & (nrows > r0))
        def _():
            fn(r0, rows)


def _group_weights(l, ie, ic, if_ref, ihn, ine, inc, hbm_refs, stg, dsts, sem, width):
    def copies(e, c):
        col = pl.multiple_of(c * width, width)
        return [pltpu.make_async_copy(h.at[e, :, pl.ds(col, width)], stg.at[k], sem.at[k])
                for k, h in enumerate(hbm_refs)]

    @pl.when(l == 0)
    def _():
        for cp in copies(ie[0], ic[0]):
            cp.start()

    @pl.when(if_ref[l] == 1)
    def _():
        for cp in copies(ie[l], ic[l]):
            cp.wait()
        for k, dst in enumerate(dsts):
            dst[...] = stg[k].astype(BF16)

        @pl.when(ihn[l] == 1)
        def _():
            for cp in copies(ine[l], inc[l]):
                cp.start()


def _moe_up_kernel(ie, ic, it, if_ref, iv_ref, in_ref, ihn, ine, inc,
                   xs_ref, wg_hbm, wu_hbm, bg_ref, bu_ref, a_ref, stg, wg_s, wu_s, sem):
    l = pl.program_id(0)
    _group_weights(l, ie, ic, if_ref, ihn, ine, inc, (wg_hbm, wu_hbm), stg, (wg_s, wu_s), sem, wg_s.shape[1])

    def sub_tile(r0, rows):
        w = xs_ref[r0:r0 + rows, :]
        row = r0 + lax.broadcasted_iota(I32, w.shape, 0)
        w = jnp.where(row < in_ref[l], w, jnp.uint32(0))
        lo, hi = _unpack_pair(w)
        x = jnp.concatenate([lo.astype(BF16), hi.astype(BF16)], axis=1)
        gate = jnp.minimum(jnp.dot(x, wg_s[...], preferred_element_type=F32) + bg_ref[...], SWIGLU_LIMIT)
        lin = jnp.clip(jnp.dot(x, wu_s[...], preferred_element_type=F32) + bu_ref[...],
                       -SWIGLU_LIMIT, SWIGLU_LIMIT)
        a_ref[r0:r0 + rows, :] = (gate * jax.nn.sigmoid(SWIGLU_ALPHA * gate) * (lin + 1.0)).astype(BF16)

    _for_sub_tiles(iv_ref[l] == 1, in_ref[l], xs_ref.shape[0], sub_tile)


def _moe_up(items, xs, w_gate, w_up, b_gate, b_up, *, bmoe, nck):
    rows, dh = xs.shape
    n_exp, d, ff = w_gate.shape
    fc = ff // nck
    n_items = items[0].shape[0]
    grid_spec = pltpu.PrefetchScalarGridSpec(
        num_scalar_prefetch=N_ITEM_ARRAYS,
        grid=(n_items,),
        in_specs=[pl.BlockSpec((bmoe, dh), lambda l, ie, ic, it, *_: (it[l], 0)),
                  pl.BlockSpec(memory_space=pl.ANY),
                  pl.BlockSpec(memory_space=pl.ANY),
                  pl.BlockSpec((None, 1, fc), lambda l, ie, ic, it, *_: (ie[l], 0, ic[l])),
                  pl.BlockSpec((None, 1, fc), lambda l, ie, ic, it, *_: (ie[l], 0, ic[l]))],
        out_specs=pl.BlockSpec((bmoe, fc), lambda l, ie, ic, it, *_: (it[l], ic[l])),
        scratch_shapes=[pltpu.VMEM((2, d, fc), F32), pltpu.VMEM((d, fc), BF16), pltpu.VMEM((d, fc), BF16),
                        pltpu.SemaphoreType.DMA((2,))],
    )
    return pl.pallas_call(
        _moe_up_kernel,
        grid_spec=grid_spec,
        out_shape=jax.ShapeDtypeStruct((rows, ff), BF16),
        compiler_params=_cparams(("arbitrary",)),
        name="moe_up",
    )(*items, xs, w_gate, w_up, b_gate, b_up)


def _moe_down_kernel(ie, ic, it, if_ref, iv_ref, in_ref, ihn, ine, inc,
                     a_ref, wd_hbm, bd_ref, y_ref, stg, wd_s, sem):
    l = pl.program_id(0)
    _group_weights(l, ie, ic, if_ref, ihn, ine, inc, (wd_hbm,), stg, (wd_s,), sem, wd_s.shape[1])

    def sub_tile(r0, rows):
        y = jnp.dot(a_ref[r0:r0 + rows, :], wd_s[...], preferred_element_type=F32) + bd_ref[...]
        half = y.shape[1] // 2
        y_ref[r0:r0 + rows, :] = _pack_pair(y[:, :half], y[:, half:])

    _for_sub_tiles(iv_ref[l] == 1, in_ref[l], a_ref.shape[0], sub_tile)


def _moe_down(items, act, w_down, b_down, *, bmoe, nck):
    rows, ff = act.shape
    n_exp, _, d = w_down.shape
    dc = d // nck
    n_items = items[0].shape[0]
    grid_spec = pltpu.PrefetchScalarGridSpec(
        num_scalar_prefetch=N_ITEM_ARRAYS,
        grid=(n_items,),
        in_specs=[pl.BlockSpec((bmoe, ff), lambda l, ie, ic, it, *_: (it[l], 0)),
                  pl.BlockSpec(memory_space=pl.ANY),
                  pl.BlockSpec((None, 1, dc), lambda l, ie, ic, it, *_: (ie[l], 0, ic[l]))],
        out_specs=pl.BlockSpec((bmoe, dc // 2), lambda l, ie, ic, it, *_: (it[l], ic[l])),
        scratch_shapes=[pltpu.VMEM((1, ff, dc), F32), pltpu.VMEM((ff, dc), BF16), pltpu.SemaphoreType.DMA((1,))],
    )
    return pl.pallas_call(
        _moe_down_kernel,
        grid_spec=grid_spec,
        out_shape=jax.ShapeDtypeStruct((rows, d // 2), U32),
        compiler_params=_cparams(("arbitrary",)),
        name="moe_down",
    )(*items, act, w_down, b_down)


def _combine_kernel(pos_ref, posn_ref, w_ref, x_ref, g_ref, ys_ref, o_ref, pos_s, buf, sem_p, sem, *, nck):
    i = pl.program_id(0)
    bc = x_ref.shape[0]
    slot = i % 2
    rc = COMBINE_ROWS

    def load_pos(p_ref):
        cp = pltpu.make_async_copy(p_ref, pos_s, sem_p)
        cp.start()
        cp.wait()

    def issue_rows(t0, n, sl):
        for t in range(t0, t0 + n):
            for s in range(TOP_K):
                dst = buf.at[sl, s, t // SUBLANES, pl.ds(t % SUBLANES, 1)]
                pltpu.make_async_copy(_sorted_row(ys_ref, pos_s, s, t), dst, sem.at[sl]).start(priority=s % 2)

    def wait_slot(sl):
        for s in range(TOP_K):
            pltpu.make_async_copy(ys_ref.at[pl.ds(0, bc // SUBLANES)], buf.at[sl, s], sem.at[sl]).wait()

    @pl.when(i == 0)
    def _():
        load_pos(pos_ref)
        issue_rows(0, bc, 0)

    load_pos(posn_ref)
    dq = x_ref.shape[1] // (2 * nck)

    def step(cur):
        wait_slot(cur)
        for r0 in range(0, bc, rc):
            issue_rows(r0, rc, 1 - cur)
            wrow = w_ref[r0:r0 + rc, :]
            ws = [jnp.broadcast_to(wrow[:, s:s + 1], (rc, LANES)) for s in range(TOP_K)]
            tiles = slice(r0 // SUBLANES, (r0 + rc) // SUBLANES)
            for k0 in range(0, nck * dq, LANES):
                lo_acc = hi_acc = None
                for s in range(TOP_K):
                    lo, hi = _unpack_pair(buf[cur, s, tiles, :, k0:k0 + LANES].reshape(rc, LANES))
                    lo_acc = ws[s] * lo if lo_acc is None else lo_acc + ws[s] * lo
                    hi_acc = ws[s] * hi if hi_acc is None else hi_acc + ws[s] * hi
                c = k0 // dq
                col = c * 2 * dq + (k0 - c * dq)
                for cc, acc in ((col, lo_acc), (col + dq, hi_acc)):
                    cs = slice(cc, cc + LANES)
                    o_ref[r0:r0 + rc, cs] = x_ref[r0:r0 + rc, cs] + g_ref[:, cs] * acc

        @pl.when(i == pl.num_programs(0) - 1)
        def _():
            wait_slot(1 - cur)

    for cur in range(2):
        pl.when(slot == cur)(functools.partial(step, cur))


def _combine(pos8, wtok, x1, gate, ys, *, seq, nck, bc):
    m, d = x1.shape
    tpb = seq // bc
    nsteps = m // bc
    kern = functools.partial(_combine_kernel, nck=nck)
    return pl.pallas_call(
        kern,
        grid=(nsteps,),
        in_specs=[pl.BlockSpec((2 * TOP_K, bc), lambda i: (0, i)),
                  pl.BlockSpec((2 * TOP_K, bc), lambda i: (0, jnp.minimum(i + 1, nsteps - 1))),
                  pl.BlockSpec((bc, 2 * TOP_K), lambda i: (i, 0)),
                  pl.BlockSpec((bc, d), lambda i: (i, 0)),
                  pl.BlockSpec((None, 1, d), lambda i: (i // tpb, 0, 0)),
                  pl.BlockSpec(memory_space=pl.ANY)],
        out_specs=pl.BlockSpec((bc, d), lambda i: (i, 0)),
        out_shape=jax.ShapeDtypeStruct((m, d), F32),
        scratch_shapes=[pltpu.SMEM((2 * TOP_K, bc), I32),
                        pltpu.VMEM((2, TOP_K, bc // SUBLANES, SUBLANES, d // 2), U32),
                        pltpu.SemaphoreType.DMA(()), pltpu.SemaphoreType.DMA((2,))],
        compiler_params=_cparams(("arbitrary",)),
        name="combine",
    )(pos8, pos8, wtok, x1, gate, ys)


def _rope_tables(seq):
    t = jnp.arange(seq)
    row = (t // GRID_W).astype(F32)
    col = (t % GRID_W).astype(F32)
    inv_freq = ROPE_BASE ** (-jnp.arange(0, ROPE_HALF, 2, dtype=F32) / ROPE_HALF)
    ar = row[:, None] * inv_freq[None, :]
    ac = col[:, None] * inv_freq[None, :]
    cos = jnp.concatenate([jnp.cos(ar), jnp.cos(ar), jnp.cos(ac), jnp.cos(ac)], axis=1)
    sin = jnp.concatenate([-jnp.sin(ar), jnp.sin(ar), -jnp.sin(ac), jnp.sin(ac)], axis=1)
    rep = LANES // HEAD_DIM
    return jnp.tile(cos, (1, rep)), jnp.tile(sin, (1, rep))


def _window_bias(seq):
    i = jnp.arange(BLOCK)[:, None]
    j = jnp.arange(3 * BLOCK)[None, :]
    band = (j >= i) & (j <= i + 2 * WINDOW)
    first = band & (j >= BLOCK)
    last = band & (j < 2 * BLOCK)
    both = first & last
    variants = jnp.stack([both if seq == BLOCK else first, band, last])
    return jnp.where(variants, 0.0, NEG_INF).astype(F32)


def _work_items(counts, *, bmoe, nck, n_tiles):
    n_exp = counts.shape[0]
    tiles_e = (counts + bmoe - 1) // bmoe
    tile_end = jnp.cumsum(tiles_e)
    tile_start = tile_end - tiles_e
    item_end = nck * tile_end
    total = item_end[-1]
    l = jnp.arange(nck * n_tiles, dtype=I32)
    valid = l < total
    lc = jnp.minimum(l, total - 1)
    e = jnp.minimum(jnp.sum((item_end[None, :] <= lc[:, None]).astype(I32), axis=1), n_exp - 1)
    is_e = e[:, None] == jnp.arange(n_exp, dtype=I32)[None, :]
    of_e = lambda v: jnp.sum(jnp.where(is_e, v[None, :], 0), axis=1)
    nt = jnp.maximum(of_e(tiles_e), 1)
    ts = of_e(tile_start)
    within = lc - nck * ts
    c = within // nt
    k = within % nt
    t = ts + k
    first = (k == 0) & valid
    nrows = jnp.clip(of_e(counts) - k * bmoe, 0, bmoe)
    e_after = jnp.sum((item_end[None, :] <= of_e(item_end)[:, None]).astype(I32), axis=1)
    same_e = c + 1 < nck
    has_next = valid & (same_e | (e_after < n_exp))
    next_e = jnp.where(same_e, e, jnp.minimum(e_after, n_exp - 1))
    next_c = jnp.where(same_e, c + 1, 0)
    as_i = lambda a: a.astype(I32)
    return (as_i(e), as_i(c), as_i(t), as_i(first), as_i(valid), as_i(nrows),
            as_i(has_next), as_i(next_e), as_i(next_c)), tile_start * bmoe


def kernel(x, c, ctx, c_ctx, w_ada, b_ada, norm1_g, w_in, q_norm_g, k_norm_g, sinks, w_pool, pool_scale,
           w_out, norm2_g, w_router, b_router, w_gate, b_gate, w_up, b_up, w_down, b_down):
    assert w_ada.shape[0] == 1, "one layer: the context stream is only read, never updated"
    batch, seq, d = x.shape
    ctx_len = ctx.shape[1]
    aw = d // 2
    nq = aw // HEAD_DIM
    nkv = max(1, nq // 8)
    grp = nq // nkv
    kvw = nkv * HEAD_DIM
    pw = d - aw
    n_exp, _, ff = w_gate.shape[1:]
    m = batch * seq
    assert seq % BLOCK == 0 and seq % GRID_W == 0 and kvw % LANES == 0 and grp % 2 == 0 and aw == pw

    c8 = jnp.zeros((8, d), F32).at[:batch].set(c).at[batch].set(c_ctx)
    mod = _ada(c8, w_ada[0], b_ada)
    mod_b = mod[:batch].reshape(batch, N_MOD, 1, d)
    mod_c = mod[batch].reshape(N_MOD, 1, d)

    w_in_b = w_in[0].astype(BF16)
    w_out_b = w_out[0].astype(BF16)
    cos_t, sin_t = _rope_tables(seq)
    qg = jnp.tile(q_norm_g[0].astype(F32), LANES // HEAD_DIM)[None, :]
    kg = jnp.tile(k_norm_g[0].astype(F32), LANES // HEAD_DIM)[None, :]
    swap16 = lambda g: g.reshape(LANES // ROPE_HALF, 2, ROPE_HALF // 2)[:, ::-1, :].reshape(1, LANES)
    qs = HEAD_DIM ** -0.5 * LOG2E
    qa, qb = cos_t * (qg * qs), sin_t * (swap16(qg) * qs)
    ka, kb = cos_t * kg, sin_t * swap16(kg)
    bn = 2 * kvw
    lane_head = jnp.arange(bn) // HEAD_DIM
    bd = jnp.where(lane_head[:, None] == lane_head[None, :], 1.0 / HEAD_DIM, 0.0).astype(BF16)
    g1 = norm1_g[0][None, :]

    x2 = x.reshape(m, d)
    q, kv, u = _inproj(x2, mod_b[:, 0], mod_b[:, 1], g1, w_in_b, qa, qb, ka, kb, bd,
                       seq=seq, aw=aw, kvw=kvw, pw=pw)
    kvc = _ctx_kv(ctx.reshape(batch * ctx_len, d), mod_c[0], mod_c[1], g1,
                  w_in_b[:, aw:aw + bn], kg, bd, kvw=kvw)
    attn = _attn(q, kv, kvc, _window_bias(seq), sinks[0].astype(F32),
                 batch=batch, seq=seq, ctx_len=ctx_len, nkv=nkv, grp=grp, kvw=kvw)
    pool = _pool(u, w_pool[0].astype(BF16), pool_scale, seq=seq)
    x1 = _outproj(attn, pool, w_out_b, x2, mod_b[:, 2], seq=seq)

    bm5 = min(512, seq)
    tri = (jnp.arange(bm5)[:, None] < jnp.arange(bm5)[None, :]).astype(BF16)
    hp, meta_i, meta_w, cnt = _router(x1, mod_b[:, 3], mod_b[:, 4], norm2_g[0][None, :],
                                      w_router[0].T.astype(BF16), b_router[0][:, None], tri, seq=seq)
    counts = cnt[:, 0].astype(I32)
    bmoe = 1024 if m * TOP_K >= 1024 * n_exp else 128
    nck_up, nck_dn = 2, 1
    n_tiles = (m * TOP_K) // bmoe + n_exp
    items_up, row_start = _work_items(counts, bmoe=bmoe, nck=nck_up, n_tiles=n_tiles)
    items_dn, _ = _work_items(counts, bmoe=bmoe, nck=nck_dn, n_tiles=n_tiles)
    idx, rank = meta_i[:TOP_K], meta_i[TOP_K:]
    start_of = jnp.sum(jnp.where(idx[..., None] == jnp.arange(n_exp), row_start, 0), axis=-1)
    pos = start_of + rank
    pos8 = jnp.concatenate([pos // SUBLANES, pos % SUBLANES], axis=0).astype(I32)

    rows = n_tiles * bmoe
    xs = _dispatch(pos8, hp, rows=rows, bd=min(256, seq)).reshape(rows, d // 2)
    act = _moe_up(items_up, xs, w_gate[0], w_up[0], b_gate[0][:, None, :], b_up[0][:, None, :],
                  bmoe=bmoe, nck=nck_up)
    ys = _moe_down(items_dn, act, w_down[0], b_down[0][:, None, :], bmoe=bmoe, nck=nck_dn)
    out = _combine(pos8, meta_w.T, x1, mod_b[:, 5], ys.reshape(rows // SUBLANES, SUBLANES, d // 2),
                   seq=seq, nck=nck_dn, bc=min(256, seq))
    return out.reshape(batch, seq, d)
```

```python
import functools

import jax
import jax.numpy as jnp
from jax import lax
from jax.experimental import pallas as pl
from jax.experimental.pallas import tpu as pltpu

F32 = jnp.float32
BF16 = jnp.bfloat16
U32 = jnp.uint32
I32 = jnp.int32

HEAD_DIM = 64
LANES = 128
GRID_W = 64
WINDOW = 128
BLOCK = 128
ROPE_HALF = HEAD_DIM // 2
ROPE_BASE = 10000.0
POOL_WINDOWS = (2, 4, 8, 16)
POOL_HALO = 8
TOP_K = 4
N_MOD = 6
NORM_EPS = 1e-6
NEG_INF = -1e30
SWIGLU_LIMIT = 7.0
SWIGLU_ALPHA = 1.702
LOG2E = 1.4426950408889634
VMEM_LIMIT = 56 * 1024 * 1024


def _cparams(sem):
    return pltpu.CompilerParams(dimension_semantics=sem, vmem_limit_bytes=VMEM_LIMIT)


def _pack_pair(lo, hi):
    a = pltpu.bitcast(lo.astype(BF16).astype(F32), U32) >> 16
    b = pltpu.bitcast(hi.astype(BF16).astype(F32), U32) & jnp.uint32(0xFFFF0000)
    return a | b


def _unpack_pair(w):
    lo = pltpu.bitcast(w << 16, F32)
    hi = pltpu.bitcast(w & jnp.uint32(0xFFFF0000), F32)
    return lo, hi


def _ada_kernel(c_ref, w_ref, b_ref, o_ref):
    c = c_ref[...]
    s = c * jax.nn.sigmoid(c)
    hi = s.astype(BF16)
    lo = (s - hi.astype(F32)).astype(BF16)
    lhs = jnp.concatenate([hi, lo], axis=0)
    r = jnp.dot(lhs, w_ref[...].astype(BF16), preferred_element_type=F32)
    o_ref[...] = r[:8] + r[8:] + b_ref[...]


def _ada(c8, w_ada, b_ada):
    d, n = w_ada.shape
    bn = min(512, n)
    return pl.pallas_call(
        _ada_kernel,
        grid=(n // bn,),
        in_specs=[pl.BlockSpec((8, d), lambda j: (0, 0)),
                  pl.BlockSpec((d, bn), lambda j: (0, j)),
                  pl.BlockSpec((1, bn), lambda j: (0, j))],
        out_specs=pl.BlockSpec((8, bn), lambda j: (0, j)),
        out_shape=jax.ShapeDtypeStruct((8, n), F32),
        compiler_params=_cparams(("parallel",)),
        name="ada",
    )(c8, w_ada, b_ada)


def _norm_mod(x, g, shift, scale):
    ms = jnp.mean(x * x, axis=-1, keepdims=True)
    y = x * lax.rsqrt(ms + NORM_EPS) * g
    return y * (1.0 + scale) + shift


def _head_rnorm(z, bd):
    msq = jnp.dot((z * z).astype(BF16), bd, preferred_element_type=F32)
    return lax.rsqrt(msq + NORM_EPS)


def _rope(z, ta, tb):
    rows = z.shape[0]
    lane = lax.broadcasted_iota(I32, (rows, LANES), 1)
    first = (lane % ROPE_HALF) < (ROPE_HALF // 2)
    outs = []
    for s in range(z.shape[1] // LANES):
        zs = z[:, s * LANES:(s + 1) * LANES]
        up = pltpu.roll(zs, LANES - ROPE_HALF // 2, 1)
        dn = pltpu.roll(zs, ROPE_HALF // 2, 1)
        outs.append(zs * ta + jnp.where(first, up, dn) * tb)
    return outs[0] if len(outs) == 1 else jnp.concatenate(outs, axis=1)


def _tile_gain(g_ref, width):
    g = g_ref[...]
    return g if width == LANES else jnp.concatenate([g] * (width // LANES), axis=1)


INPROJ_NORM_ROWS = 256


def _inproj_kernel(x_hbm, sh_ref, sc_ref, g1_ref, w_ref, qa_ref, qb_ref, ka_ref, kb_ref, bd_ref,
                   q_ref, kv_ref, u_ref, xbuf, h_ref, zs_ref, sem, *, nq_t, kvw):
    i = pl.program_id(0)
    j = pl.program_id(1)
    cur = j % 2
    bm = xbuf.shape[0]

    def mm():
        return jnp.dot(h_ref[...], w_ref[...], preferred_element_type=F32)

    def x_copy(rb):
        return pltpu.make_async_copy(x_hbm.at[pl.ds(pl.multiple_of(rb * bm, bm), bm)], xbuf, sem)

    @pl.when((i == 0) & (j == 0))
    def _():
        x_copy(0).start()

    @pl.when(j == 0)
    def _():
        x_copy(i).wait()
        rows = min(INPROJ_NORM_ROWS, bm)
        for r in range(0, bm, rows):
            h_ref[r:r + rows, :] = _norm_mod(xbuf[r:r + rows, :], g1_ref[...], sh_ref[...],
                                             sc_ref[...]).astype(BF16)

        @pl.when(i + 1 < pl.num_programs(0))
        def _():
            x_copy(i + 1).start()

        zs_ref[0] = mm()

    @pl.when((j >= 1) & (j <= nq_t))
    def _():
        zp = zs_ref[1 - cur]
        zs_ref[cur] = mm()
        q_ref[...] = (_rope(zp, qa_ref[...], qb_ref[...]) * _head_rnorm(zp, bd_ref[...])).astype(BF16)

    @pl.when(j == nq_t + 1)
    def _():
        zp = zs_ref[1 - cur]
        u_ref[...] = mm()
        zk = zp[:, :kvw]
        kv_ref[:, :kvw] = (_rope(zk, ka_ref[...], kb_ref[...]) * _head_rnorm(zk, bd_ref[:kvw, :kvw])).astype(BF16)
        kv_ref[:, kvw:] = zp[:, kvw:].astype(BF16)

    @pl.when(j > nq_t + 1)
    def _():
        u_ref[...] = mm()


def _inproj(x2, shift, scale, g1, w_in_b, qa, qb, ka, kb, bd, *, seq, aw, kvw, pw):
    m, d = x2.shape
    bn = 2 * kvw
    nq_t, nu_t = aw // bn, pw // bn
    bm = min(1024, seq)
    tpb = seq // bm
    kern = functools.partial(_inproj_kernel, nq_t=nq_t, kvw=kvw)
    return pl.pallas_call(
        kern,
        grid=(m // bm, nq_t + 1 + nu_t),
        in_specs=[pl.BlockSpec(memory_space=pl.ANY),
                  pl.BlockSpec((None, 1, d), lambda i, j: (i // tpb, 0, 0)),
                  pl.BlockSpec((None, 1, d), lambda i, j: (i // tpb, 0, 0)),
                  pl.BlockSpec((1, d), lambda i, j: (0, 0)),
                  pl.BlockSpec((d, bn), lambda i, j: (0, j)),
                  pl.BlockSpec((bm, LANES), lambda i, j: (i % tpb, 0)),
                  pl.BlockSpec((bm, LANES), lambda i, j: (i % tpb, 0)),
                  pl.BlockSpec((bm, LANES), lambda i, j: (i % tpb, 0)),
                  pl.BlockSpec((bm, LANES), lambda i, j: (i % tpb, 0)),
                  pl.BlockSpec((bn, bn), lambda i, j: (0, 0))],
        out_specs=[pl.BlockSpec((bm, bn), lambda i, j: (i, jnp.clip(j - 1, 0, nq_t - 1))),
                   pl.BlockSpec((bm, bn), lambda i, j: (i, 0)),
                   pl.BlockSpec((bm, bn), lambda i, j: (i, jnp.clip(j - nq_t - 1, 0, nu_t - 1)))],
        out_shape=[jax.ShapeDtypeStruct((m, aw), BF16),
                   jax.ShapeDtypeStruct((m, 2 * kvw), BF16),
                   jax.ShapeDtypeStruct((m, pw), F32)],
        scratch_shapes=[pltpu.VMEM((bm, d), F32), pltpu.VMEM((bm, d), BF16), pltpu.VMEM((2, bm, bn), F32),
                        pltpu.SemaphoreType.DMA(())],
        compiler_params=_cparams(("arbitrary", "arbitrary")),
        name="inproj",
    )(x2, shift, scale, g1, w_in_b, qa, qb, ka, kb, bd)


def _ctx_kv_kernel(x_ref, sh_ref, sc_ref, g1_ref, w_ref, kg_ref, bd_ref, kv_ref, *, kvw):
    h = _norm_mod(x_ref[...], g1_ref[...], sh_ref[...], sc_ref[...]).astype(BF16)
    z = jnp.dot(h, w_ref[...], preferred_element_type=F32)
    zk = z[:, :kvw]
    kv_ref[:, :kvw] = (zk * _head_rnorm(zk, bd_ref[:kvw, :kvw]) * _tile_gain(kg_ref, kvw)).astype(BF16)
    kv_ref[:, kvw:] = z[:, kvw:].astype(BF16)


def _ctx_kv(c2, shift, scale, g1, w_kv_b, kg, bd, *, kvw):
    m, d = c2.shape
    bn = 2 * kvw
    bm = min(256, m)
    kern = functools.partial(_ctx_kv_kernel, kvw=kvw)
    return pl.pallas_call(
        kern,
        grid=(m // bm,),
        in_specs=[pl.BlockSpec((bm, d), lambda i: (i, 0)),
                  pl.BlockSpec((1, d), lambda i: (0, 0)),
                  pl.BlockSpec((1, d), lambda i: (0, 0)),
                  pl.BlockSpec((1, d), lambda i: (0, 0)),
                  pl.BlockSpec((d, bn), lambda i: (0, 0)),
                  pl.BlockSpec((1, LANES), lambda i: (0, 0)),
                  pl.BlockSpec((bn, bn), lambda i: (0, 0))],
        out_specs=pl.BlockSpec((bm, bn), lambda i: (i, 0)),
        out_shape=jax.ShapeDtypeStruct((m, bn), BF16),
        compiler_params=_cparams(("parallel",)),
        name="ctx_kv",
    )(c2, shift, scale, g1, w_kv_b, kg, bd)


def _attn_kernel(sink_ref, q_ref, kp_ref, kc_ref, kn_ref, kx_ref, bias_ref, o_ref, *, nkv, grp, kvw, ctx_len):
    npair = grp // 2
    nk = ctx_len + 3 * BLOCK
    nq = npair * BLOCK
    lane_kv = lax.broadcasted_iota(I32, (nk, LANES), 1)
    row_o = lax.broadcasted_iota(I32, (LANES, nq), 0)
    bias = bias_ref[...]
    for h in range(nkv):
        slab = (h // 2) * LANES
        mine = (lane_kv // HEAD_DIM) == (h % 2)

        def window(off):
            w = jnp.concatenate([kx_ref[:, off + slab:off + slab + LANES],
                                 kp_ref[:, off + slab:off + slab + LANES],
                                 kc_ref[:, off + slab:off + slab + LANES],
                                 kn_ref[:, off + slab:off + slab + LANES]], axis=0).astype(F32)
            wz = jnp.where(mine, w, 0.0)
            wr = pltpu.roll(wz, HEAD_DIM, 1)
            lo, hi = (wz, wr) if h % 2 == 0 else (wr, wz)
            return lo.astype(BF16), hi.astype(BF16)

        k_e, k_o = window(0)
        v_e, v_o = window(kvw)
        q0 = h * grp * HEAD_DIM
        q4 = jnp.concatenate([q_ref[:, q0 + p * LANES:q0 + (p + 1) * LANES] for p in range(npair)], axis=0)
        kcat = jnp.concatenate([k_e, k_o], axis=0)
        s_all = lax.dot_general(kcat, q4, (((1,), (1,)), ((), ())), preferred_element_type=F32)
        probs, rinv = [], []
        for half in range(2):
            sink = sink_ref[2 * h + half]
            s = s_all[half * nk:(half + 1) * nk, :]
            own = ctx_len + BLOCK
            s = jnp.concatenate([s[:ctx_len], s[ctx_len:own] + bias[:BLOCK], s[own:own + BLOCK],
                                 s[own + BLOCK:] + bias[2 * BLOCK:]], axis=0)
            mx = jnp.maximum(jnp.max(s, axis=0, keepdims=True), sink)
            e = jnp.exp2(s - mx)
            den = jnp.sum(e, axis=0, keepdims=True) + jnp.exp2(sink - mx)
            probs.append(e.astype(BF16))
            rinv.append(1.0 / den)
        tn = (((0,), (0,)), ((), ()))
        o_t = (lax.dot_general(v_e, probs[0], tn, preferred_element_type=F32)
               + lax.dot_general(v_o, probs[1], tn, preferred_element_type=F32))
        o = (o_t * jnp.where(row_o < HEAD_DIM, rinv[0], rinv[1])).T
        for p in range(npair):
            o_ref[:, q0 + p * LANES:q0 + (p + 1) * LANES] = o[p * BLOCK:(p + 1) * BLOCK].astype(BF16)


def _attn(q, kv, kvc, bias, sinks, *, batch, seq, ctx_len, nkv, grp, kvw):
    m, aw = q.shape
    nb = seq // BLOCK
    kern = functools.partial(_attn_kernel, nkv=nkv, grp=grp, kvw=kvw, ctx_len=ctx_len)

    def sel(n):
        first = 1 - jnp.minimum(n, 1)
        last = 1 - jnp.minimum(nb - 1 - n, 1)
        return 1 - first + last

    npair = grp // 2
    nq = npair * BLOCK
    sink_rows = jnp.repeat((sinks * LOG2E).reshape(nkv, npair, 2).transpose(0, 2, 1), BLOCK,
                           axis=2).reshape(2 * nkv, 1, nq)
    bias_t = jnp.tile(bias.transpose(0, 2, 1), (1, 1, npair))
    return pl.pallas_call(
        kern,
        grid=(batch, nb),
        in_specs=[pl.BlockSpec((2 * nkv, 1, nq), lambda b, n: (0, 0, 0)),
                  pl.BlockSpec((BLOCK, aw), lambda b, n: (b * nb + n, 0)),
                  pl.BlockSpec((BLOCK, 2 * kvw), lambda b, n: (b * nb + jnp.maximum(n - 1, 0), 0)),
                  pl.BlockSpec((BLOCK, 2 * kvw), lambda b, n: (b * nb + n, 0)),
                  pl.BlockSpec((BLOCK, 2 * kvw), lambda b, n: (b * nb + jnp.minimum(n + 1, nb - 1), 0)),
                  pl.BlockSpec((ctx_len, 2 * kvw), lambda b, n: (b, 0)),
                  pl.BlockSpec((None, 3 * BLOCK, nq), lambda b, n: (sel(n), 0, 0))],
        out_specs=pl.BlockSpec((BLOCK, aw), lambda b, n: (b * nb + n, 0)),
        out_shape=jax.ShapeDtypeStruct((m, aw), BF16),
        compiler_params=_cparams(("parallel", "parallel")),
        name="attn",
    )(sink_rows, q, kv, kv, kv, kvc, bias_t)


def _pool_kernel(up_ref, uc_ref, un_ref, wp_ref, ps_ref, o_ref, *, seq, pgw):
    bm = uc_ref.shape[0]
    t0 = (pl.program_id(0) * bm) % seq
    rows = bm + 2 * POOL_HALO
    pt = t0 + lax.broadcasted_iota(I32, (bm, rows), 0)
    ps = t0 - POOL_HALO + lax.broadcasted_iota(I32, (bm, rows), 1)
    pt1 = t0 + lax.broadcasted_iota(I32, (bm, 1), 0)
    for g, w in enumerate(POOL_WINDOWS):
        cs = slice(g * pgw, (g + 1) * pgw)
        uc = uc_ref[:, cs]
        ucat = jnp.concatenate([up_ref[:, cs], uc, un_ref[:, cs]], axis=0)
        lo = jnp.clip(pt - w // 2, 0, seq)
        hi = jnp.clip(pt + w // 2, 0, seq)
        band = jnp.where((ps >= lo) & (ps < hi), 1.0, 0.0).astype(BF16)
        cnt = (jnp.clip(pt1 + w // 2, 0, seq) - jnp.clip(pt1 - w // 2, 0, seq)).astype(F32)
        tot = jnp.dot(band, ucat.astype(BF16), preferred_element_type=F32)
        dlt = tot / cnt - uc
        y = jnp.dot(dlt.astype(BF16), wp_ref[g], preferred_element_type=F32)
        o_ref[:, cs] = (y * ps_ref[:, cs]).astype(BF16)


def _pool(u, w_pool_b, pool_scale, *, seq):
    m, pw = u.shape
    pgw = pw // len(POOL_WINDOWS)
    bm = min(256, seq)
    hb = bm // POOL_HALO
    nh = m // POOL_HALO
    kern = functools.partial(_pool_kernel, seq=seq, pgw=pgw)
    return pl.pallas_call(
        kern,
        grid=(m // bm,),
        in_specs=[pl.BlockSpec((POOL_HALO, pw), lambda i: (jnp.maximum(i * hb - 1, 0), 0)),
                  pl.BlockSpec((bm, pw), lambda i: (i, 0)),
                  pl.BlockSpec((POOL_HALO, pw), lambda i: (jnp.minimum((i + 1) * hb, nh - 1), 0)),
                  pl.BlockSpec((len(POOL_WINDOWS), pgw, pgw), lambda i: (0, 0, 0)),
                  pl.BlockSpec((1, pw), lambda i: (0, 0))],
        out_specs=pl.BlockSpec((bm, pw), lambda i: (i, 0)),
        out_shape=jax.ShapeDtypeStruct((m, pw), BF16),
        compiler_params=_cparams(("parallel",)),
        name="pool",
    )(u, u, u, w_pool_b, pool_scale)


def _outproj_kernel(a_ref, p_ref, wa_ref, wp_ref, x_ref, g_ref, o_ref):
    mix = (jnp.dot(a_ref[...], wa_ref[...], preferred_element_type=F32)
           + jnp.dot(p_ref[...], wp_ref[...], preferred_element_type=F32))
    o_ref[...] = x_ref[...] + g_ref[...] * mix


def _outproj(attn, pool, w_out_b, x2, gate, *, seq):
    m, d = x2.shape
    aw = attn.shape[1]
    bm = min(1024, seq)
    bn = min(1024, d)
    tpb = seq // bm
    return pl.pallas_call(
        _outproj_kernel,
        grid=(m // bm, d // bn),
        in_specs=[pl.BlockSpec((bm, aw), lambda i, j: (i, 0)),
                  pl.BlockSpec((bm, aw), lambda i, j: (i, 0)),
                  pl.BlockSpec((aw, bn), lambda i, j: (0, j)),
                  pl.BlockSpec((aw, bn), lambda i, j: (1, j)),
                  pl.BlockSpec((bm, bn), lambda i, j: (i, j)),
                  pl.BlockSpec((None, 1, bn), lambda i, j: (i // tpb, 0, j))],
        out_specs=pl.BlockSpec((bm, bn), lambda i, j: (i, j)),
        out_shape=jax.ShapeDtypeStruct((m, d), F32),
        compiler_params=_cparams(("parallel", "parallel")),
        name="outproj",
    )(attn, pool, w_out_b, w_out_b, x2, gate)


def _router_kernel(x_ref, sh_ref, sc_ref, g2_ref, wr_ref, br_ref, tri_ref,
                   hp_ref, mi_ref, mw_ref, cnt_ref, carry_ref, *, n_exp):
    i = pl.program_id(0)

    @pl.when(i == 0)
    def _():
        carry_ref[...] = jnp.zeros_like(carry_ref)

    h = _norm_mod(x_ref[...], g2_ref[...], sh_ref[...], sc_ref[...])
    bm, d = h.shape
    hp_ref[...] = _pack_pair(h[:, :d // 2], h[:, d // 2:])
    logits = lax.dot_general(wr_ref[...], h.astype(BF16), (((1,), (1,)), ((), ())),
                             preferred_element_type=F32) + br_ref[...]
    eidx = lax.broadcasted_iota(I32, (n_exp, bm), 0)
    l = logits
    tops, idxs, sels = [], [], []
    for _ in range(TOP_K):
        mx = jnp.max(l, axis=0, keepdims=True)
        ix = jnp.min(jnp.where(l == mx, eidx, n_exp), axis=0, keepdims=True)
        sel = eidx == ix
        tops.append(mx)
        idxs.append(ix)
        sels.append(sel)
        l = jnp.where(sel, -jnp.inf, l)
    es = [jnp.exp(t - tops[0]) for t in tops]
    den = es[0] + es[1] + es[2] + es[3]
    member = jnp.where(sels[0] | sels[1] | sels[2] | sels[3], 1.0, 0.0)
    before = jnp.dot(member.astype(BF16), tri_ref[...], preferred_element_type=F32) + carry_ref[:, 0:1]
    ranks = [jnp.sum(jnp.where(s, before, 0.0), axis=0, keepdims=True).astype(I32) for s in sels]
    carry_ref[...] = carry_ref[...] + jnp.sum(member, axis=1, keepdims=True)
    mi_ref[...] = jnp.concatenate(idxs + ranks, axis=0)
    mw_ref[...] = jnp.concatenate([e / den for e in es] + [jnp.zeros((TOP_K, bm), F32)], axis=0)
    cnt_ref[...] = carry_ref[...]


def _router(x1, shift, scale, g2, w_rt_b, b_r, tri, *, seq):
    m, d = x1.shape
    n_exp = w_rt_b.shape[0]
    bm = tri.shape[0]
    tpb = seq // bm
    kern = functools.partial(_router_kernel, n_exp=n_exp)
    return pl.pallas_call(
        kern,
        grid=(m // bm,),
        in_specs=[pl.BlockSpec((bm, d), lambda i: (i, 0)),
                  pl.BlockSpec((None, 1, d), lambda i: (i // tpb, 0, 0)),
                  pl.BlockSpec((None, 1, d), lambda i: (i // tpb, 0, 0)),
                  pl.BlockSpec((1, d), lambda i: (0, 0)),
                  pl.BlockSpec((n_exp, d), lambda i: (0, 0)),
                  pl.BlockSpec((n_exp, 1), lambda i: (0, 0)),
                  pl.BlockSpec((bm, bm), lambda i: (0, 0))],
        out_specs=[pl.BlockSpec((bm, d // 2), lambda i: (i, 0)),
                   pl.BlockSpec((2 * TOP_K, bm), lambda i: (0, i)),
                   pl.BlockSpec((2 * TOP_K, bm), lambda i: (0, i)),
                   pl.BlockSpec((n_exp, LANES), lambda i: (0, 0))],
        out_shape=[jax.ShapeDtypeStruct((m, d // 2), U32),
                   jax.ShapeDtypeStruct((2 * TOP_K, m), I32),
                   jax.ShapeDtypeStruct((2 * TOP_K, m), F32),
                   jax.ShapeDtypeStruct((n_exp, LANES), F32)],
        scratch_shapes=[pltpu.VMEM((n_exp, LANES), F32)],
        compiler_params=_cparams(("arbitrary",)),
        name="router",
    )(x1, shift, scale, g2, w_rt_b, b_r, tri)


SUBLANES = 8
COMBINE_ROWS = 32


def _sorted_row(ref3, pos_s, s, t):
    return ref3.at[pos_s[s, t], pl.ds(pos_s[TOP_K + s, t], 1)]


def _dispatch_kernel(pos_ref, h_ref, xs_ref, pos_s, sem_p, sem):
    bd = h_ref.shape[0]
    cp = pltpu.make_async_copy(pos_ref, pos_s, sem_p)
    cp.start()
    cp.wait()
    copies = [pltpu.make_async_copy(h_ref.at[pl.ds(t, 1)], _sorted_row(xs_ref, pos_s, s, t), sem)
              for t in range(bd) for s in range(TOP_K)]
    for k, cp in enumerate(copies):
        cp.start(priority=k % 2)
    for cp in copies:
        cp.wait()


def _dispatch(pos8, hp, *, rows, bd):
    m, dh = hp.shape
    return pl.pallas_call(
        _dispatch_kernel,
        grid=(m // bd,),
        in_specs=[pl.BlockSpec((2 * TOP_K, bd), lambda i: (0, i)),
                  pl.BlockSpec((bd, dh), lambda i: (i, 0))],
        out_specs=pl.BlockSpec(memory_space=pl.ANY),
        out_shape=jax.ShapeDtypeStruct((rows // SUBLANES, SUBLANES, dh), U32),
        scratch_shapes=[pltpu.SMEM((2 * TOP_K, bd), I32), pltpu.SemaphoreType.DMA(()),
                        pltpu.SemaphoreType.DMA(())],
        compiler_params=_cparams(("arbitrary",)),
        name="dispatch",
    )(pos8, hp)


N_ITEM_ARRAYS = 9


MOE_SUB_ROWS = 256


def _for_sub_tiles(valid, nrows, block_rows, fn):
    rows = min(MOE_SUB_ROWS, block_rows)
    full = nrows >= block_rows
    pl.when(valid & full)(functools.partial(fn, 0, block_rows))
    for r0 in range(0, block_rows, rows):
        pl.when(valid & (nrows > r0) & jnp.logical_not(full))(functools.partial(fn, r0, rows))


def _group_weights(l, ie, ic, if_ref, ihn, ine, inc, hbm_refs, stg, dsts, sem, width):
    def copies(e, c):
        col = pl.multiple_of(c * width, width)
        return [pltpu.make_async_copy(h.at[e, :, pl.ds(col, width)], stg.at[k], sem.at[k])
                for k, h in enumerate(hbm_refs)]

    @pl.when(l == 0)
    def _():
        for cp in copies(ie[0], ic[0]):
            cp.start()

    @pl.when(if_ref[l] == 1)
    def _():
        for cp in copies(ie[l], ic[l]):
            cp.wait()
        for k, dst in enumerate(dsts):
            dst[...] = stg[k].astype(BF16)

        @pl.when(ihn[l] == 1)
        def _():
            for cp in copies(ine[l], inc[l]):
                cp.start()


def _moe_up_kernel(ie, ic, it, if_ref, iv_ref, in_ref, ihn, ine, inc,
                   xs_ref, wg_hbm, wu_hbm, bg_ref, bu_ref, a_ref, stg, wg_s, wu_s, sem):
    l = pl.program_id(0)
    _group_weights(l, ie, ic, if_ref, ihn, ine, inc, (wg_hbm, wu_hbm), stg, (wg_s, wu_s), sem, wg_s.shape[1])

    def sub_tile(r0, rows):
        w = xs_ref[r0:r0 + rows, :]
        row = r0 + lax.broadcasted_iota(I32, w.shape, 0)
        w = jnp.where(row < in_ref[l], w, jnp.uint32(0))
        lo, hi = _unpack_pair(w)
        x = jnp.concatenate([lo.astype(BF16), hi.astype(BF16)], axis=1)
        gate = jnp.minimum(jnp.dot(x, wg_s[...], preferred_element_type=F32) + bg_ref[...], SWIGLU_LIMIT)
        lin = jnp.clip(jnp.dot(x, wu_s[...], preferred_element_type=F32) + bu_ref[...],
                       -SWIGLU_LIMIT, SWIGLU_LIMIT)
        a_ref[r0:r0 + rows, :] = (gate * jax.nn.sigmoid(SWIGLU_ALPHA * gate) * (lin + 1.0)).astype(BF16)

    _for_sub_tiles(iv_ref[l] == 1, in_ref[l], xs_ref.shape[0], sub_tile)


def _moe_up(items, xs, w_gate, w_up, b_gate, b_up, *, bmoe, nck):
    rows, dh = xs.shape
    n_exp, d, ff = w_gate.shape
    fc = ff // nck
    n_items = items[0].shape[0]
    grid_spec = pltpu.PrefetchScalarGridSpec(
        num_scalar_prefetch=N_ITEM_ARRAYS,
        grid=(n_items,),
        in_specs=[pl.BlockSpec((bmoe, dh), lambda l, ie, ic, it, *_: (it[l], 0)),
                  pl.BlockSpec(memory_space=pl.ANY),
                  pl.BlockSpec(memory_space=pl.ANY),
                  pl.BlockSpec((None, 1, fc), lambda l, ie, ic, it, *_: (ie[l], 0, ic[l])),
                  pl.BlockSpec((None, 1, fc), lambda l, ie, ic, it, *_: (ie[l], 0, ic[l]))],
        out_specs=pl.BlockSpec((bmoe, fc), lambda l, ie, ic, it, *_: (it[l], ic[l])),
        scratch_shapes=[pltpu.VMEM((2, d, fc), F32), pltpu.VMEM((d, fc), BF16), pltpu.VMEM((d, fc), BF16),
                        pltpu.SemaphoreType.DMA((2,))],
    )
    return pl.pallas_call(
        _moe_up_kernel,
        grid_spec=grid_spec,
        out_shape=jax.ShapeDtypeStruct((rows, ff), BF16),
        compiler_params=_cparams(("arbitrary",)),
        name="moe_up",
    )(*items, xs, w_gate, w_up, b_gate, b_up)


def _moe_down_kernel(ie, ic, it, if_ref, iv_ref, in_ref, ihn, ine, inc,
                     a_ref, wd_hbm, bd_ref, y_ref, stg, wd_s, sem):
    l = pl.program_id(0)
    _group_weights(l, ie, ic, if_ref, ihn, ine, inc, (wd_hbm,), stg, (wd_s,), sem, wd_s.shape[1])

    def sub_tile(r0, rows):
        y = jnp.dot(a_ref[r0:r0 + rows, :], wd_s[...], preferred_element_type=F32) + bd_ref[...]
        half = y.shape[1] // 2
        y_ref[r0:r0 + rows, :] = _pack_pair(y[:, :half], y[:, half:])

    _for_sub_tiles(iv_ref[l] == 1, in_ref[l], a_ref.shape[0], sub_tile)


def _moe_down(items, act, w_down, b_down, *, bmoe, nck):
    rows, ff = act.shape
    n_exp, _, d = w_down.shape
    dc = d // nck
    n_items = items[0].shape[0]
    grid_spec = pltpu.PrefetchScalarGridSpec(
        num_scalar_prefetch=N_ITEM_ARRAYS,
        grid=(n_items,),
        in_specs=[pl.BlockSpec((bmoe, ff), lambda l, ie, ic, it, *_: (it[l], 0)),
                  pl.BlockSpec(memory_space=pl.ANY),
                  pl.BlockSpec((None, 1, dc), lambda l, ie, ic, it, *_: (ie[l], 0, ic[l]))],
        out_specs=pl.BlockSpec((bmoe, dc // 2), lambda l, ie, ic, it, *_: (it[l], ic[l])),
        scratch_shapes=[pltpu.VMEM((1, ff, dc), F32), pltpu.VMEM((ff, dc), BF16), pltpu.SemaphoreType.DMA((1,))],
    )
    return pl.pallas_call(
        _moe_down_kernel,
        grid_spec=grid_spec,
        out_shape=jax.ShapeDtypeStruct((rows, d // 2), U32),
        compiler_params=_cparams(("arbitrary",)),
        name="moe_down",
    )(*items, act, w_down, b_down)


def _combine_kernel(pos_ref, posn_ref, w_ref, x_ref, g_ref, ys_ref, o_ref, pos_s, buf, sem_p, sem, *, nck):
    i = pl.program_id(0)
    bc = x_ref.shape[0]
    slot = i % 2
    rc = COMBINE_ROWS

    def load_pos(p_ref):
        cp = pltpu.make_async_copy(p_ref, pos_s, sem_p)
        cp.start()
        cp.wait()

    def issue_rows(t0, n, sl):
        for t in range(t0, t0 + n):
            for s in range(TOP_K):
                dst = buf.at[sl, s, t // SUBLANES, pl.ds(t % SUBLANES, 1)]
                pltpu.make_async_copy(_sorted_row(ys_ref, pos_s, s, t), dst, sem.at[sl]).start(priority=s % 2)

    def wait_slot(sl):
        for s in range(TOP_K):
            pltpu.make_async_copy(ys_ref.at[pl.ds(0, bc // SUBLANES)], buf.at[sl, s], sem.at[sl]).wait()

    @pl.when(i == 0)
    def _():
        load_pos(pos_ref)
        issue_rows(0, bc, 0)

    load_pos(posn_ref)
    dq = x_ref.shape[1] // (2 * nck)

    def step(cur):
        wait_slot(cur)
        for r0 in range(0, bc, rc):
            issue_rows(r0, rc, 1 - cur)
            wrow = w_ref[r0:r0 + rc, :]
            ws = [jnp.broadcast_to(wrow[:, s:s + 1], (rc, LANES)) for s in range(TOP_K)]
            tiles = slice(r0 // SUBLANES, (r0 + rc) // SUBLANES)
            for k0 in range(0, nck * dq, LANES):
                lo_acc = hi_acc = None
                for s in range(TOP_K):
                    lo, hi = _unpack_pair(buf[cur, s, tiles, :, k0:k0 + LANES].reshape(rc, LANES))
                    lo_acc = ws[s] * lo if lo_acc is None else lo_acc + ws[s] * lo
                    hi_acc = ws[s] * hi if hi_acc is None else hi_acc + ws[s] * hi
                c = k0 // dq
                col = c * 2 * dq + (k0 - c * dq)
                for cc, acc in ((col, lo_acc), (col + dq, hi_acc)):
                    cs = slice(cc, cc + LANES)
                    o_ref[r0:r0 + rc, cs] = x_ref[r0:r0 + rc, cs] + g_ref[:, cs] * acc

        @pl.when(i == pl.num_programs(0) - 1)
        def _():
            wait_slot(1 - cur)

    for cur in range(2):
        pl.when(slot == cur)(functools.partial(step, cur))


def _combine(pos8, wtok, x1, gate, ys, *, seq, nck, bc):
    m, d = x1.shape
    tpb = seq // bc
    nsteps = m // bc
    kern = functools.partial(_combine_kernel, nck=nck)
    return pl.pallas_call(
        kern,
        grid=(nsteps,),
        in_specs=[pl.BlockSpec((2 * TOP_K, bc), lambda i: (0, i)),
                  pl.BlockSpec((2 * TOP_K, bc), lambda i: (0, jnp.minimum(i + 1, nsteps - 1))),
                  pl.BlockSpec((bc, 2 * TOP_K), lambda i: (i, 0)),
                  pl.BlockSpec((bc, d), lambda i: (i, 0)),
                  pl.BlockSpec((None, 1, d), lambda i: (i // tpb, 0, 0)),
                  pl.BlockSpec(memory_space=pl.ANY)],
        out_specs=pl.BlockSpec((bc, d), lambda i: (i, 0)),
        out_shape=jax.ShapeDtypeStruct((m, d), F32),
        scratch_shapes=[pltpu.SMEM((2 * TOP_K, bc), I32),
                        pltpu.VMEM((2, TOP_K, bc // SUBLANES, SUBLANES, d // 2), U32),
                        pltpu.SemaphoreType.DMA(()), pltpu.SemaphoreType.DMA((2,))],
        compiler_params=_cparams(("arbitrary",)),
        name="combine",
    )(pos8, pos8, wtok, x1, gate, ys)


def _rope_tables(seq):
    t = jnp.arange(seq)
    row = (t // GRID_W).astype(F32)
    col = (t % GRID_W).astype(F32)
    inv_freq = ROPE_BASE ** (-jnp.arange(0, ROPE_HALF, 2, dtype=F32) / ROPE_HALF)
    ar = row[:, None] * inv_freq[None, :]
    ac = col[:, None] * inv_freq[None, :]
    cos = jnp.concatenate([jnp.cos(ar), jnp.cos(ar), jnp.cos(ac), jnp.cos(ac)], axis=1)
    sin = jnp.concatenate([-jnp.sin(ar), jnp.sin(ar), -jnp.sin(ac), jnp.sin(ac)], axis=1)
    rep = LANES // HEAD_DIM
    return jnp.tile(cos, (1, rep)), jnp.tile(sin, (1, rep))


def _window_bias(seq):
    i = jnp.arange(BLOCK)[:, None]
    j = jnp.arange(3 * BLOCK)[None, :]
    band = (j >= i) & (j <= i + 2 * WINDOW)
    first = band & (j >= BLOCK)
    last = band & (j < 2 * BLOCK)
    both = first & last
    variants = jnp.stack([both if seq == BLOCK else first, band, last])
    return jnp.where(variants, 0.0, NEG_INF).astype(F32)


def _work_items(counts, *, bmoe, nck, n_tiles):
    n_exp = counts.shape[0]
    tiles_e = (counts + bmoe - 1) // bmoe
    tile_end = jnp.cumsum(tiles_e)
    tile_start = tile_end - tiles_e
    item_end = nck * tile_end
    total = item_end[-1]
    l = jnp.arange(nck * n_tiles, dtype=I32)
    valid = l < total
    lc = jnp.minimum(l, total - 1)
    e = jnp.minimum(jnp.sum((item_end[None, :] <= lc[:, None]).astype(I32), axis=1), n_exp - 1)
    is_e = e[:, None] == jnp.arange(n_exp, dtype=I32)[None, :]
    of_e = lambda v: jnp.sum(jnp.where(is_e, v[None, :], 0), axis=1)
    nt = jnp.maximum(of_e(tiles_e), 1)
    ts = of_e(tile_start)
    within = lc - nck * ts
    c = within // nt
    k = within % nt
    t = ts + k
    first = (k == 0) & valid
    nrows = jnp.clip(of_e(counts) - k * bmoe, 0, bmoe)
    e_after = jnp.sum((item_end[None, :] <= of_e(item_end)[:, None]).astype(I32), axis=1)
    same_e = c + 1 < nck
    has_next = valid & (same_e | (e_after < n_exp))
    next_e = jnp.where(same_e, e, jnp.minimum(e_after, n_exp - 1))
    next_c = jnp.where(same_e, c + 1, 0)
    as_i = lambda a: a.astype(I32)
    return (as_i(e), as_i(c), as_i(t), as_i(first), as_i(valid), as_i(nrows),
            as_i(has_next), as_i(next_e), as_i(next_c)), tile_start * bmoe


def kernel(x, c, ctx, c_ctx, w_ada, b_ada, norm1_g, w_in, q_norm_g, k_norm_g, sinks, w_pool, pool_scale,
           w_out, norm2_g, w_router, b_router, w_gate, b_gate, w_up, b_up, w_down, b_down):
    assert w_ada.shape[0] == 1, "one layer: the context stream is only read, never updated"
    batch, seq, d = x.shape
    ctx_len = ctx.shape[1]
    aw = d // 2
    nq = aw // HEAD_DIM
    nkv = max(1, nq // 8)
    grp = nq // nkv
    kvw = nkv * HEAD_DIM
    pw = d - aw
    n_exp, _, ff = w_gate.shape[1:]
    m = batch * seq
    assert seq % BLOCK == 0 and seq % GRID_W == 0 and kvw % LANES == 0 and grp % 2 == 0 and aw == pw

    c8 = jnp.zeros((8, d), F32).at[:batch].set(c).at[batch].set(c_ctx)
    mod = _ada(c8, w_ada[0], b_ada)
    mod_b = mod[:batch].reshape(batch, N_MOD, 1, d)
    mod_c = mod[batch].reshape(N_MOD, 1, d)

    w_in_b = w_in[0].astype(BF16)
    w_out_b = w_out[0].astype(BF16)
    cos_t, sin_t = _rope_tables(seq)
    qg = jnp.tile(q_norm_g[0].astype(F32), LANES // HEAD_DIM)[None, :]
    kg = jnp.tile(k_norm_g[0].astype(F32), LANES // HEAD_DIM)[None, :]
    swap16 = lambda g: g.reshape(LANES // ROPE_HALF, 2, ROPE_HALF // 2)[:, ::-1, :].reshape(1, LANES)
    qs = HEAD_DIM ** -0.5 * LOG2E
    qa, qb = cos_t * (qg * qs), sin_t * (swap16(qg) * qs)
    ka, kb = cos_t * kg, sin_t * swap16(kg)
    bn = 2 * kvw
    lane_head = jnp.arange(bn) // HEAD_DIM
    bd = jnp.where(lane_head[:, None] == lane_head[None, :], 1.0 / HEAD_DIM, 0.0).astype(BF16)
    g1 = norm1_g[0][None, :]

    x2 = x.reshape(m, d)
    q, kv, u = _inproj(x2, mod_b[:, 0], mod_b[:, 1], g1, w_in_b, qa, qb, ka, kb, bd,
                       seq=seq, aw=aw, kvw=kvw, pw=pw)
    kvc = _ctx_kv(ctx.reshape(batch * ctx_len, d), mod_c[0], mod_c[1], g1,
                  w_in_b[:, aw:aw + bn], kg, bd, kvw=kvw)
    attn = _attn(q, kv, kvc, _window_bias(seq), sinks[0].astype(F32),
                 batch=batch, seq=seq, ctx_len=ctx_len, nkv=nkv, grp=grp, kvw=kvw)
    pool = _pool(u, w_pool[0].astype(BF16), pool_scale, seq=seq)
    x1 = _outproj(attn, pool, w_out_b, x2, mod_b[:, 2], seq=seq)

    bm5 = min(512, seq)
    tri = (jnp.arange(bm5)[:, None] < jnp.arange(bm5)[None, :]).astype(BF16)
    hp, meta_i, meta_w, cnt = _router(x1, mod_b[:, 3], mod_b[:, 4], norm2_g[0][None, :],
                                      w_router[0].T.astype(BF16), b_router[0][:, None], tri, seq=seq)
    counts = cnt[:, 0].astype(I32)
    bmoe = 512 if m * TOP_K >= 512 * n_exp else 128
    nck_up, nck_dn = 2, 1
    n_tiles = (m * TOP_K) // bmoe + n_exp
    items_up, row_start = _work_items(counts, bmoe=bmoe, nck=nck_up, n_tiles=n_tiles)
    items_dn, _ = _work_items(counts, bmoe=bmoe, nck=nck_dn, n_tiles=n_tiles)
    idx, rank = meta_i[:TOP_K], meta_i[TOP_K:]
    start_of = jnp.sum(jnp.where(idx[..., None] == jnp.arange(n_exp), row_start, 0), axis=-1)
    pos = start_of + rank
    pos8 = jnp.concatenate([pos // SUBLANES, pos % SUBLANES], axis=0).astype(I32)

    rows = n_tiles * bmoe
    xs = _dispatch(pos8, hp, rows=rows, bd=min(256, seq)).reshape(rows, d // 2)
    act = _moe_up(items_up, xs, w_gate[0], w_up[0], b_gate[0][:, None, :], b_up[0][:, None, :],
                  bmoe=bmoe, nck=nck_up)
    ys = _moe_down(items_dn, act, w_down[0], b_down[0][:, None, :], bmoe=bmoe, nck=nck_dn)
    out = _combine(pos8, meta_w.T, x1, mod_b[:, 5], ys.reshape(rows // SUBLANES, SUBLANES, d // 2),
                   seq=seq, nck=nck_dn, bc=min(256, seq))
    return out.reshape(batch, seq, d)
```
